```python
import math
import jax, jax.numpy as jnp
from jax import lax
import numpy as np

D_MODEL = 1024
BATCH = 8
SEQ = 8192
DEPTH = 2

SSM_D_INNER = D_MODEL
SSM_HEAD_DIM = 64
SSM_HEADS = SSM_D_INNER // SSM_HEAD_DIM
SSM_GROUPS = 4
SSM_D_STATE = 128
SSM_CONV = 4
SSM_CHUNK = 128
SSM_CONV_DIM = SSM_D_INNER + 2 * SSM_GROUPS * SSM_D_STATE
SB_HEAD_DIM = 128
SB_HEADS = 4
SB_QBLK = 128
NSA_HEAD_DIM = 64
NSA_Q_HEADS = 8
NSA_KV_HEADS = 2
CMP_STRIDE = 16
CMP_LEN = 2 * CMP_STRIDE
SEL_BLOCK = 64
N_SEL = 8
WINDOW = 512
NSA_QBLK = 32
D_FF = 11 * D_MODEL // 4
FFN_CONV = 3

NORM_EPS = 1e-6
NEG = -1e30
BIG = 1e30

IN_SPLIT_SIZES = (SSM_D_INNER, SSM_CONV_DIM, SSM_HEADS, 3 * SB_HEADS * SB_HEAD_DIM,
                  NSA_Q_HEADS * NSA_HEAD_DIM, 6 * NSA_KV_HEADS * NSA_HEAD_DIM, 3 * NSA_Q_HEADS, 3 * D_MODEL)
D_IN = sum(IN_SPLIT_SIZES)

kernel_name = "hybrid_ssd_stickbreak_nsa_block"


def rms_norm(x, w):
    xf = x.astype(jnp.float32)
    y = xf * lax.rsqrt(jnp.mean(xf * xf, axis=-1, keepdims=True) + NORM_EPS)
    return (y * w.astype(jnp.float32)).astype(x.dtype)


def causal_dwconv(x, w, b):
    k, c = w.shape
    out = lax.conv_general_dilated(x, w[:, None, :].astype(x.dtype), window_strides=(1,),
                                   padding=[(k - 1, 0)], dimension_numbers=('NWC', 'WIO', 'NWC'),
                                   feature_group_count=c)
    return out + b.astype(x.dtype)


def masked_softmax(s, mask):
    p = jax.nn.softmax(jnp.where(mask, s, NEG), axis=-1)
    return jnp.where(mask, p, 0.0)


def alibi_slopes(n):
    return jnp.asarray(2.0 ** (-8.0 * np.arange(1, n + 1) / n), dtype=jnp.float32)


def split_points():
    return np.cumsum(np.array(IN_SPLIT_SIZES))[:-1].tolist()


def ssd_chunked(x, dt, a, bmat, cmat):
    bsz, s, h, p = x.shape
    g, n = bmat.shape[2], bmat.shape[3]
    hg = h // g
    nc = s // SSM_CHUNK

    def chunks(t, tail):
        return jnp.moveaxis(t.reshape((bsz, nc, SSM_CHUNK) + tail), 1, 0)

    xc = chunks(x, (g, hg, p))
    dtc = chunks(dt, (g, hg))
    bc = chunks(bmat, (g, n))
    cc = chunks(cmat, (g, n))
    a = a.reshape(g, hg)
    tri = jnp.tril(jnp.ones((SSM_CHUNK, SSM_CHUNK), dtype=bool))[None, :, :, None, None]

    def step(state, inp):
        x_c, dt_c, b_c, c_c = inp
        acum = jnp.cumsum(dt_c * a, axis=1)
        seg = acum[:, :, None] - acum[:, None, :]
        decay = jnp.exp(jnp.where(tri, seg, -jnp.inf))
        cb = jnp.einsum('btgn,bsgn->btsg', c_c, b_c)
        w = cb[..., None] * decay * dt_c[:, None]
        y = jnp.einsum('btsgh,bsghp->btghp', w, x_c)
        y = y + jnp.einsum('btgn,bghpn->btghp', c_c, state) * jnp.exp(acum)[..., None]
        to_end = jnp.exp(acum[:, -1:] - acum) * dt_c
        state = (state * jnp.exp(acum[:, -1])[..., None, None]
                 + jnp.einsum('bsgh,bsghp,bsgn->bghpn', to_end, x_c, b_c))
        return state, y

    state0 = jnp.zeros((bsz, g, hg, p, n), jnp.float32)
    _, ys = lax.scan(step, state0, (xc, dtc, bc, cc))
    return jnp.moveaxis(ys, 0, 1).reshape(bsz, s, h, p)


def ssm_mixer(z, xbc, dt_raw, conv_w, conv_b, dt_bias, a_log, d_skip, norm_w):
    bsz, s, _ = z.shape
    xbc = jax.nn.silu(causal_dwconv(xbc, conv_w, conv_b))
    xs, bm, cm = jnp.split(xbc, [SSM_D_INNER, SSM_D_INNER + SSM_GROUPS * SSM_D_STATE], axis=-1)
    xs = xs.reshape(bsz, s, SSM_HEADS, SSM_HEAD_DIM).astype(jnp.float32)
    bm = bm.reshape(bsz, s, SSM_GROUPS, SSM_D_STATE).astype(jnp.float32)
    cm = cm.reshape(bsz, s, SSM_GROUPS, SSM_D_STATE).astype(jnp.float32)
    dt = jax.nn.softplus(dt_raw.astype(jnp.float32) + dt_bias.astype(jnp.float32))
    a = -jnp.exp(a_log.astype(jnp.float32))
    y = ssd_chunked(xs, dt, a, bm, cm) + d_skip.astype(jnp.float32)[:, None] * xs
    y = y.reshape(bsz, s, SSM_D_INNER) * jax.nn.silu(z.astype(jnp.float32))
    yg = y.reshape(bsz, s, SSM_GROUPS, SSM_D_INNER // SSM_GROUPS)
    yg = yg * lax.rsqrt(jnp.mean(yg * yg, axis=-1, keepdims=True) + NORM_EPS)
    return (yg.reshape(bsz, s, SSM_D_INNER) * norm_w.astype(jnp.float32)).astype(z.dtype)


def stick_breaking_attention(qkv):
    bsz, s, _ = qkv.shape
    q, k, v = [jnp.moveaxis(t.reshape(bsz, s, SB_HEADS, SB_HEAD_DIM), 2, 1)
               for t in jnp.split(qkv, 3, axis=-1)]
    scale = SB_HEAD_DIM ** -0.5
    outs = []
    for i in range(s // SB_QBLK):
        q0 = i * SB_QBLK
        kl = q0 + SB_QBLK
        qb = q[:, :, q0:kl]
        zl = jnp.einsum('bhqd,bhkd->bhqk', qb, k[:, :, :kl]).astype(jnp.float32) * scale
        mask = jnp.arange(kl)[None, :] < (q0 + jnp.arange(SB_QBLK))[:, None]
        ls = jax.nn.log_sigmoid(zl)
        lk = jnp.where(mask, ls - zl, 0.0)
        later = lax.cumsum(lk, axis=3, reverse=True) - lk
        w = jnp.where(mask, jnp.exp(ls + later), 0.0)
        outs.append(jnp.einsum('bhqk,bhkd->bhqd', w.astype(v.dtype), v[:, :, :kl]))
    out = jnp.concatenate(outs, axis=2)
    return out.transpose(0, 2, 1, 3).reshape(bsz, s, SB_HEADS * SB_HEAD_DIM)


def compress_blocks(kraw, pos_emb, w1, w2):
    bsz, s, g, dh = kraw.shape
    chunks = kraw.reshape(bsz, s // CMP_STRIDE, CMP_STRIDE, g, dh)
    blocks = jnp.concatenate([chunks[:, :-1], chunks[:, 1:]], axis=2) + pos_emb[None, None, :, None, :].astype(kraw.dtype)
    blocks = blocks.transpose(0, 1, 3, 2, 4).reshape(bsz, s // CMP_STRIDE - 1, g, CMP_LEN * dh)
    return jax.nn.silu(blocks @ w1) @ w2


def nsa_attention(q, kv, gate, cmp_pos_k, cmp_w1_k, cmp_w2_k, cmp_pos_v, cmp_w1_v, cmp_w2_v):
    bsz, s, _ = q.shape
    g, hg, dh = NSA_KV_HEADS, NSA_Q_HEADS // NSA_KV_HEADS, NSA_HEAD_DIM
    q = q.reshape(bsz, s, g, hg, dh)
    k_cmp, v_cmp, k_sel, v_sel, k_win, v_win = [t.reshape(bsz, s, g, dh) for t in jnp.split(kv, 6, axis=-1)]
    gate = jax.nn.sigmoid(gate.astype(jnp.float32)).reshape(bsz, s, 3, g, hg)
    kc = compress_blocks(k_cmp, cmp_pos_k, cmp_w1_k, cmp_w2_k)
    vc = compress_blocks(v_cmp, cmp_pos_v, cmp_w1_v, cmp_w2_v)
    nc = kc.shape[1]
    cmp_end = jnp.arange(nc) * CMP_STRIDE + CMP_LEN - 1
    nb = s // SEL_BLOCK
    n_sel = min(N_SEL, nb)
    ratio = SEL_BLOCK // CMP_STRIDE
    ks_blk = k_sel.reshape(bsz, nb, SEL_BLOCK, g, dh).transpose(0, 3, 1, 2, 4)
    vs_blk = v_sel.reshape(bsz, nb, SEL_BLOCK, g, dh).transpose(0, 3, 1, 2, 4)
    kw = jnp.pad(k_win, ((0, 0), (WINDOW, 0), (0, 0), (0, 0)))
    vw = jnp.pad(v_win, ((0, 0), (WINDOW, 0), (0, 0), (0, 0)))
    slopes = alibi_slopes(NSA_Q_HEADS).reshape(g, hg)
    scale = dh ** -0.5
    bi = jnp.arange(bsz)[:, None, None, None]
    gi = jnp.arange(g)[None, :, None, None]
    blk_ids = jnp.arange(nb)

    def block(q0):
        qb = lax.dynamic_slice_in_dim(q, q0, NSA_QBLK, axis=1)
        gb = lax.dynamic_slice_in_dim(gate, q0, NSA_QBLK, axis=1)
        tpos = q0 + jnp.arange(NSA_QBLK)
        dist_c = tpos[:, None] - cmp_end[None, :]
        sc = (jnp.einsum('bqghd,bngd->bghqn', qb, kc).astype(jnp.float32) * scale
              - slopes[:, :, None, None] * dist_c.astype(jnp.float32))
        p_cmp = masked_softmax(sc, dist_c >= 0)
        o_cmp = jnp.einsum('bghqn,bngd->bqghd', p_cmp.astype(vc.dtype), vc)
        imp_pad = jnp.pad(p_cmp.sum(axis=2), ((0, 0), (0, 0), (0, 0), (1, 0)))
        imp = (imp_pad.reshape(bsz, g, NSA_QBLK, nb, ratio).sum(axis=-1)
               + jnp.pad(imp_pad[..., ratio::ratio], ((0, 0), (0, 0), (0, 0), (0, 1))))
        cur = tpos // SEL_BLOCK
        forced = ((blk_ids[None, :] == 0) | (blk_ids[None, :] == cur[:, None])
                  | (blk_ids[None, :] == cur[:, None] - 1))
        valid = blk_ids[None, :] * SEL_BLOCK <= tpos[:, None]
        imp = jnp.where(valid, jnp.where(forced, BIG, imp), NEG)
        _, idx = lax.top_k(imp, n_sel)
        ks = ks_blk[bi, gi, idx]
        vs = vs_blk[bi, gi, idx]
        pos = idx[..., None] * SEL_BLOCK + jnp.arange(SEL_BLOCK)
        dist_s = tpos[None, None, :, None, None] - pos
        ss = (jnp.einsum('bqghd,bgqnkd->bghqnk', qb, ks).astype(jnp.float32) * scale
              - slopes[None, :, :, None, None, None] * dist_s[:, :, None].astype(jnp.float32))
        ss = ss.reshape(bsz, g, hg, NSA_QBLK, n_sel * SEL_BLOCK)
        p_sel = masked_softmax(ss, (dist_s >= 0).reshape(bsz, g, 1, NSA_QBLK, n_sel * SEL_BLOCK))
        o_sel = jnp.einsum('bghqm,bgqmd->bqghd', p_sel.astype(vs.dtype),
                           vs.reshape(bsz, g, NSA_QBLK, n_sel * SEL_BLOCK, dh))
        kwb = lax.dynamic_slice_in_dim(kw, q0, WINDOW + NSA_QBLK, axis=1)
        vwb = lax.dynamic_slice_in_dim(vw, q0, WINDOW + NSA_QBLK, axis=1)
        wpos = q0 - WINDOW + jnp.arange(WINDOW + NSA_QBLK)
        dist_w = tpos[:, None] - wpos[None, :]
        wmask = (dist_w >= 0) & (dist_w < WINDOW) & (wpos[None, :] >= 0)
        sw = (jnp.einsum('bqghd,bkgd->bghqk', qb, kwb).astype(jnp.float32) * scale
              - slopes[:, :, None, None] * dist_w.astype(jnp.float32))
        o_win = jnp.einsum('bghqk,bkgd->bqghd', masked_softmax(sw, wmask).astype(vwb.dtype), vwb)
        out = (gb[:, :, 0, :, :, None] * o_cmp + gb[:, :, 1, :, :, None] * o_sel
               + gb[:, :, 2, :, :, None] * o_win)
        return out.astype(q.dtype)

    out = lax.map(block, jnp.arange(0, s, NSA_QBLK))
    return jnp.moveaxis(out, 0, 1).reshape(bsz, s, NSA_Q_HEADS * NSA_HEAD_DIM)


def setup_inputs(seed: int = 0) -> dict:
    key = jax.random.key(seed)
    ks = iter(jax.random.split(key, 40))
    L = DEPTH
    f32 = jnp.float32

    def nrm(shape, scale):
        return jax.random.normal(next(ks), shape, f32) * scale

    def gain(shape):
        return 1.0 + nrm(shape, 0.02)

    dt0 = jnp.exp(jax.random.uniform(next(ks), (L, SSM_HEADS), f32)
                  * (math.log(0.1) - math.log(1e-3)) + math.log(1e-3))
    dt_bias = dt0 + jnp.log(-jnp.expm1(-dt0))
    a_log = jnp.log(jax.random.uniform(next(ks), (L, SSM_HEADS), f32, minval=1.0, maxval=16.0))
    return {
        "x": nrm((BATCH, SEQ, D_MODEL), 1.0),
        "pre_mix_norm": gain((L, D_MODEL)),
        "w_in": nrm((L, D_MODEL, D_IN), D_MODEL ** -0.5),
        "ssm_conv_w": nrm((L, SSM_CONV, SSM_CONV_DIM), SSM_CONV ** -0.5),
        "ssm_conv_b": nrm((L, SSM_CONV_DIM), 0.01),
        "ssm_dt_bias": dt_bias,
        "ssm_a_log": a_log,
        "ssm_d": 1.0 + nrm((L, SSM_HEADS), 0.1),
        "ssm_norm": gain((L, SSM_D_INNER)),
        "cmp_pos_k": nrm((L, CMP_LEN, NSA_HEAD_DIM), 0.02),
        "cmp_w1_k": nrm((L, CMP_LEN * NSA_HEAD_DIM, NSA_HEAD_DIM), (CMP_LEN * NSA_HEAD_DIM) ** -0.5),
        "cmp_w2_k": nrm((L, NSA_HEAD_DIM, NSA_HEAD_DIM), NSA_HEAD_DIM ** -0.5),
        "cmp_pos_v": nrm((L, CMP_LEN, NSA_HEAD_DIM), 0.02),
        "cmp_w1_v": nrm((L, CMP_LEN * NSA_HEAD_DIM, NSA_HEAD_DIM), (CMP_LEN * NSA_HEAD_DIM) ** -0.5),
        "cmp_w2_v": nrm((L, NSA_HEAD_DIM, NSA_HEAD_DIM), NSA_HEAD_DIM ** -0.5),
        "w_br_ssm": nrm((L, SSM_D_INNER, D_MODEL), SSM_D_INNER ** -0.5),
        "w_br_sb": nrm((L, SB_HEADS * SB_HEAD_DIM, D_MODEL), (SB_HEADS * SB_HEAD_DIM) ** -0.5),
        "w_br_nsa": nrm((L, NSA_Q_HEADS * NSA_HEAD_DIM, D_MODEL), (NSA_Q_HEADS * NSA_HEAD_DIM) ** -0.5),
        "w_out": nrm((L, D_MODEL, D_MODEL), D_MODEL ** -0.5),
        "post_mix_norm": gain((L, D_MODEL)),
        "pre_ffn_norm": gain((L, D_MODEL)),
        "ffn_w_up": nrm((L, D_MODEL, 2 * D_FF), D_MODEL ** -0.5),
        "ffn_conv_w": nrm((L, FFN_CONV, 2 * D_FF), FFN_CONV ** -0.5),
        "ffn_conv_b": nrm((L, 2 * D_FF), 0.01),
        "ffn_w_down": nrm((L, D_FF, D_MODEL), D_FF ** -0.5),
        "post_ffn_norm": gain((L, D_MODEL)),
    }


def reference(x, pre_mix_norm, w_in, ssm_conv_w, ssm_conv_b, ssm_dt_bias, ssm_a_log, ssm_d, ssm_norm,
              cmp_pos_k, cmp_w1_k, cmp_w2_k, cmp_pos_v, cmp_w1_v, cmp_w2_v,
              w_br_ssm, w_br_sb, w_br_nsa, w_out, post_mix_norm, pre_ffn_norm,
              ffn_w_up, ffn_conv_w, ffn_conv_b, ffn_w_down, post_ffn_norm):
    bsz, s, _ = x.shape
    for l in range(DEPTH):
        h = rms_norm(x, pre_mix_norm[l])
        proj = h @ w_in[l]
        z, xbc, dt_raw, sb_qkv, nsa_q, nsa_kv, nsa_gate, merge_gate = jnp.split(proj, split_points(), axis=-1)
        y_ssm = ssm_mixer(z, xbc, dt_raw, ssm_conv_w[l], ssm_conv_b[l], ssm_dt_bias[l],
                          ssm_a_log[l], ssm_d[l], ssm_norm[l])
        y_sb = stick_breaking_attention(sb_qkv)
        y_nsa = nsa_attention(nsa_q, nsa_kv, nsa_gate, cmp_pos_k[l], cmp_w1_k[l], cmp_w2_k[l],
                              cmp_pos_v[l], cmp_w1_v[l], cmp_w2_v[l])
        gm = jax.nn.sigmoid(merge_gate.astype(jnp.float32)).reshape(bsz, s, 3, D_MODEL)
        mixed = (gm[:, :, 0] * (y_ssm @ w_br_ssm[l]) + gm[:, :, 1] * (y_sb @ w_br_sb[l])
                 + gm[:, :, 2] * (y_nsa @ w_br_nsa[l])).astype(x.dtype)
        x = x + rms_norm(mixed @ w_out[l], post_mix_norm[l])
        h = rms_norm(x, pre_ffn_norm[l])
        u = causal_dwconv(h @ ffn_w_up[l], ffn_conv_w[l], ffn_conv_b[l])
        gate, val = jnp.split(u, 2, axis=-1)
        f = (jax.nn.gelu(gate, approximate=True) * val) @ ffn_w_down[l]
        x = x + rms_norm(f, post_ffn_norm[l])
    return x
```

```python
import functools

import jax
import jax.numpy as jnp
from jax import lax
from jax.experimental import pallas as pl
from jax.experimental.pallas import tpu as pltpu

F32 = jnp.float32
BF16 = jnp.bfloat16

D_MODEL = 1024
SSM_D_INNER = 1024
SSM_HEAD_DIM = 64
SSM_HEADS = 16
SSM_GROUPS = 4
SSM_D_STATE = 128
SSM_CONV = 4
SSM_CHUNK = 128
SB_HEAD_DIM = 128
SB_HEADS = 4
NSA_HEAD_DIM = 64
NSA_Q_HEADS = 8
NSA_KV_HEADS = 2
NSA_HG = NSA_Q_HEADS // NSA_KV_HEADS
CMP_STRIDE = 16
CMP_LEN = 32
SEL_BLOCK = 64
N_SEL = 8
WINDOW = 512
D_FF = 2816
FFN_CONV = 3
NORM_EPS = 1e-6
NEG = -1e30
BIG = 1e30

LANES = 128
SUBLANES = 8
VMEM_LIMIT = 56 * 1024 * 1024

SB_CUTOFF = 104.0

C_Z, C_XS, C_BC, C_MG, C_NQ, C_NKV, C_SB, C_CMP = 0, 1024, 2048, 3072, 6144, 7168, 8192, 9728
N_MAIN = 9984
N_SMALL = 384


def _dot(a, b):
    return jnp.dot(a, b, preferred_element_type=F32)


def _dot_nt(a, b):
    return lax.dot_general(a, b, (((1,), (1,)), ((), ())), preferred_element_type=F32)


def _split2(x):
    hi = x.astype(BF16)
    lo = (x - hi.astype(F32)).astype(BF16)
    return hi, lo


def _split3(x):
    hi = x.astype(BF16)
    r = x - hi.astype(F32)
    mid = r.astype(BF16)
    lo = (r - mid.astype(F32)).astype(BF16)
    return hi, mid, lo


def _sigmoid(x):
    return 1.0 / (1.0 + jnp.exp(-x))


def _softplus(x):
    return jnp.maximum(x, 0.0) + jnp.log1p(jnp.exp(-jnp.abs(x)))


def _params(*sem):
    return pltpu.CompilerParams(dimension_semantics=sem, vmem_limit_bytes=VMEM_LIMIT)


def _norm_matmul_body(x_ref, g_ref, w_ref, o_ref, h_ref):
    @pl.when(pl.program_id(1) == 0)
    def _():
        x = x_ref[...]
        r = lax.rsqrt(jnp.mean(x * x, axis=-1, keepdims=True) + NORM_EPS)
        h_ref[...] = (x * r * g_ref[...]).astype(BF16)

    o_ref[...] = _dot(h_ref[...], w_ref[...]).astype(o_ref.dtype)


def _norm_matmul(x2d, gain, w, out_dtype, tm, tn):
    t, d = x2d.shape
    n = w.shape[1]
    return pl.pallas_call(
        _norm_matmul_body,
        grid=(t // tm, n // tn),
        in_specs=[pl.BlockSpec((tm, d), lambda i, j: (i, 0)),
                  pl.BlockSpec((1, d), lambda i, j: (0, 0)),
                  pl.BlockSpec((d, tn), lambda i, j: (0, j))],
        out_specs=pl.BlockSpec((tm, tn), lambda i, j: (i, j)),
        out_shape=jax.ShapeDtypeStruct((t, n), out_dtype),
        scratch_shapes=[pltpu.VMEM((tm, d), BF16)],
        compiler_params=_params("parallel", "arbitrary"),
        name="norm_matmul",
    )(x2d, gain.reshape(1, d).astype(F32), w)


def _ssd_body(z_ref, xs_ref, bc_ref, sm_ref, cw_ref, cb_ref, dtb_ref, alog_ref, dexp_ref, nw_ref,
              o_ref, xbuf, state, ybuf):
    q = SSM_CHUNK
    c = pl.program_id(1)

    @pl.when(c == 0)
    def _():
        xbuf[0:SUBLANES, :] = jnp.zeros((SUBLANES, 2 * SSM_D_INNER), F32)
        state[...] = jnp.zeros_like(state)

    @pl.when(c > 0)
    def _():
        xbuf[0:SUBLANES, :] = xbuf[q:q + SUBLANES, :]

    xbuf[SUBLANES:SUBLANES + q, 0:SSM_D_INNER] = xs_ref[0].astype(F32)
    xbuf[SUBLANES:SUBLANES + q, SSM_D_INNER:] = bc_ref[0].astype(F32)
    conv = cb_ref[...]
    for k in range(SSM_CONV):
        off = SUBLANES - (SSM_CONV - 1) + k
        conv = conv + cw_ref[k:k + 1, :] * xbuf[off:off + q, :]
    xbc = conv * _sigmoid(conv)
    xs = xbc[:, :SSM_D_INNER]
    bm = xbc[:, SSM_D_INNER:SSM_D_INNER + SSM_GROUPS * SSM_D_STATE].astype(BF16)
    cm = xbc[:, SSM_D_INNER + SSM_GROUPS * SSM_D_STATE:].astype(BF16)

    dt = _softplus(sm_ref[0] + dtb_ref[...])
    a = -jnp.exp(alog_ref[...])
    dta = dt * a
    row = lax.broadcasted_iota(jnp.int32, (q, q), 0)
    col = lax.broadcasted_iota(jnp.int32, (q, q), 1)
    tri = row >= col
    tril = jnp.where(tri, 1.0, 0.0).astype(BF16)
    d1, d2, d3 = _split3(dta)
    acum = _dot(tril, d1) + _dot(tril, d2) + _dot(tril, d3)
    acum_t = acum.T
    dt_t = dt.T
    exp_acum = jnp.exp(acum)
    exp_last = exp_acum[q - 1:q, :]
    to_end_t = jnp.exp(acum_t[:, q - 1:q] - acum_t) * dt_t
    xs_t = xs.T
    xs_b = xs.astype(BF16)

    for g in range(SSM_GROUPS):
        bg = bm[:, g * SSM_D_STATE:(g + 1) * SSM_D_STATE]
        cg = cm[:, g * SSM_D_STATE:(g + 1) * SSM_D_STATE]
        cb = _dot_nt(cg, bg)
        for hh in range(SSM_HEADS // SSM_GROUPS):
            h = g * (SSM_HEADS // SSM_GROUPS) + hh
            lo, hi = h * SSM_HEAD_DIM, (h + 1) * SSM_HEAD_DIM
            seg = acum[:, h:h + 1] - acum_t[h:h + 1, :]
            decay = jnp.exp(jnp.where(tri, seg, NEG))
            w = (cb * decay * dt_t[h:h + 1, :]).astype(BF16)
            st = state[h]
            y = _dot(w, xs_b[:, lo:hi])
            y = y + _dot_nt(cg, st.astype(BF16)) * exp_acum[:, h:h + 1]
            ybuf[:, lo:hi] = y
            xw = (xs_t[lo:hi, :] * to_end_t[h:h + 1, :]).astype(BF16)
            state[h] = st * exp_last[:, h:h + 1] + _dot(xw, bg)

    y = ybuf[...] + dexp_ref[...] * xs
    z = z_ref[0].astype(F32)
    y = y * (z * _sigmoid(z))
    gw = SSM_D_INNER // SSM_GROUPS
    for g in range(SSM_GROUPS):
        yg = y[:, g * gw:(g + 1) * gw]
        r = lax.rsqrt(jnp.mean(yg * yg, axis=-1, keepdims=True) + NORM_EPS)
        o_ref[0, :, g * gw:(g + 1) * gw] = (yg * r * nw_ref[:, g * gw:(g + 1) * gw]).astype(o_ref.dtype)


def _ssd(proj, small, conv_w, conv_b, dt_bias, a_log, d_skip, norm_w):
    b, s, _ = proj.shape
    q = SSM_CHUNK
    wd = SSM_D_INNER
    pad = LANES - SSM_HEADS
    dtb = jnp.pad(dt_bias.astype(F32), (0, pad)).reshape(1, LANES)
    alog = jnp.pad(a_log.astype(F32), (0, pad)).reshape(1, LANES)
    dexp = jnp.repeat(d_skip.astype(F32), SSM_HEAD_DIM).reshape(1, wd)
    const = lambda shape: pl.BlockSpec(shape, lambda i, j: (0, 0))
    return pl.pallas_call(
        _ssd_body,
        grid=(b, s // q),
        in_specs=[pl.BlockSpec((1, q, wd), lambda i, j: (i, j, C_Z // wd)),
                  pl.BlockSpec((1, q, wd), lambda i, j: (i, j, C_XS // wd)),
                  pl.BlockSpec((1, q, wd), lambda i, j: (i, j, C_BC // wd)),
                  pl.BlockSpec((1, q, LANES), lambda i, j: (i, j, 0)),
                  const((SSM_CONV, 2 * wd)), const((1, 2 * wd)), const((1, LANES)), const((1, LANES)),
                  const((1, wd)), const((1, wd))],
        out_specs=pl.BlockSpec((1, q, wd), lambda i, j: (i, j, 0)),
        out_shape=jax.ShapeDtypeStruct((b, s, wd), BF16),
        scratch_shapes=[pltpu.VMEM((SUBLANES + q, 2 * wd), F32),
                        pltpu.VMEM((SSM_HEADS, SSM_HEAD_DIM, SSM_D_STATE), F32),
                        pltpu.VMEM((q, wd), F32)],
        compiler_params=_params("parallel", "arbitrary"),
        name="ssd",
    )(proj, proj, proj, small, conv_w.astype(F32), conv_b.reshape(1, -1).astype(F32), dtb, alog, dexp,
      norm_w.reshape(1, wd).astype(F32))


def _sb_body(q_ref, k_ref, v_ref, u_ref, o_ref, *, tq, tk):
    i = pl.program_id(2)
    q = q_ref[0]
    scale = SB_HEAD_DIM ** -0.5
    qpos = i * tq + lax.broadcasted_iota(jnp.int32, (tq, tk), 0)
    kiota = lax.broadcasted_iota(jnp.int32, (tq, tk), 1)
    u = u_ref[...]

    def cond(st):
        j, carry, _ = st
        return jnp.logical_and(j >= 0, jnp.max(carry) > -SB_CUTOFF)

    def body(st):
        j, carry, acc = st
        off = pl.multiple_of(j * tk, tk)
        k = k_ref[0, pl.ds(off, tk), :]
        v = v_ref[0, pl.ds(off, tk), :]
        z = _dot_nt(q, k) * scale
        sp = _softplus(z)
        mask = (kiota + j * tk) < qpos
        lk = jnp.where(mask, -sp, 0.0)
        hi, lo = _split2(lk)
        later = _dot(hi, u) + _dot(lo, u) + carry
        w = jnp.where(mask, jnp.exp(z - sp + later), 0.0).astype(BF16)
        acc = acc + _dot(w, v)
        carry = later[:, 0:1] + lk[:, 0:1]
        return j - 1, carry, acc

    init = (i * tq // tk + (tq // tk - 1), jnp.zeros((tq, 1), F32), jnp.zeros((tq, SB_HEAD_DIM), F32))
    _, _, acc = lax.while_loop(cond, body, init)
    o_ref[0] = acc.astype(o_ref.dtype)


def _sb_attention(proj, tq=256, tk=256):
    b, s, _ = proj.shape
    hd = SB_HEAD_DIM
    c0 = C_SB // hd
    r = lax.broadcasted_iota(jnp.int32, (tk, tk), 0)
    c = lax.broadcasted_iota(jnp.int32, (tk, tk), 1)
    u = (r > c).astype(BF16)
    return pl.pallas_call(
        functools.partial(_sb_body, tq=tq, tk=tk),
        grid=(b, SB_HEADS, s // tq),
        in_specs=[pl.BlockSpec((1, tq, hd), lambda i, h, j: (i, j, c0 + h)),
                  pl.BlockSpec((1, s, hd), lambda i, h, j: (i, 0, c0 + SB_HEADS + h)),
                  pl.BlockSpec((1, s, hd), lambda i, h, j: (i, 0, c0 + 2 * SB_HEADS + h)),
                  pl.BlockSpec((tk, tk), lambda i, h, j: (0, 0))],
        out_specs=pl.BlockSpec((1, tq, hd), lambda i, h, j: (i, j, h)),
        out_shape=jax.ShapeDtypeStruct((b, s, SB_HEADS * hd), BF16),
        compiler_params=_params("parallel", "parallel", "arbitrary"),
        name="sb",
    )(proj, proj, proj, u)


def _nsa_compress_body(a_ref, an_ref, w1_ref, pos_ref, w2_ref, o_ref):
    half = CMP_STRIDE * NSA_HEAD_DIM
    w1 = w1_ref[0]
    pos = pos_ref[0]
    for g in range(NSA_KV_HEADS):
        a = a_ref[0, 0, g].astype(F32)
        an = an_ref[0, 0, g].astype(F32)
        pre = (_dot((a + pos[:, :half]).astype(BF16), w1[:half])
               + _dot((an + pos[:, half:]).astype(BF16), w1[half:]))
        act = (pre * _sigmoid(pre)).astype(BF16)
        o_ref[0, 0, g] = _dot(act, w2_ref[0]).astype(o_ref.dtype)


def _nsa_compress(proj, pos_k, w1_k, w2_k, pos_v, w1_v, w2_v):
    b, s, _ = proj.shape
    n16 = s // CMP_STRIDE
    hd = NSA_HEAD_DIM
    half = CMP_STRIDE * hd
    raw = proj[:, :, C_CMP:C_CMP + 2 * NSA_KV_HEADS * hd]
    a = raw.reshape(b, n16, CMP_STRIDE, 2, NSA_KV_HEADS, hd).transpose(0, 3, 4, 1, 2, 5)
    a = a.reshape(b, 2, NSA_KV_HEADS, n16, half)
    a_next = jnp.roll(a, -1, axis=3)
    w1 = jnp.stack([w1_k, w1_v]).astype(BF16)
    pos = jnp.stack([pos_k, pos_v]).astype(F32).reshape(2, 1, CMP_LEN * hd)
    w2 = jnp.pad(jnp.stack([w2_k, w2_v]), ((0, 0), (0, 0), (0, LANES - hd))).astype(BF16)
    blk = pl.BlockSpec((1, 1, NSA_KV_HEADS, n16, half), lambda i, p: (i, p, 0, 0, 0))
    return pl.pallas_call(
        _nsa_compress_body,
        grid=(b, 2),
        in_specs=[blk, blk,
                  pl.BlockSpec((1, CMP_LEN * hd, hd), lambda i, p: (p, 0, 0)),
                  pl.BlockSpec((1, 1, CMP_LEN * hd), lambda i, p: (p, 0, 0)),
                  pl.BlockSpec((1, hd, LANES), lambda i, p: (p, 0, 0))],
        out_specs=pl.BlockSpec((1, 1, NSA_KV_HEADS, n16, LANES), lambda i, p: (i, p, 0, 0, 0)),
        out_shape=jax.ShapeDtypeStruct((b, 2, NSA_KV_HEADS, n16, LANES), BF16),
        compiler_params=_params("parallel", "parallel"),
        name="nsa_compress",
    )(a, a_next, w1, pos, w2)


def _nsa_cmp_body(q_ref, kc_ref, vc_ref, pool_ref, oc_ref, sel_ref, *, tq, n16, nb, n_sel):
    i = pl.program_id(1)
    scale = NSA_HEAD_DIM ** -0.5
    tpos = i * tq + lax.broadcasted_iota(jnp.int32, (tq, 1), 0)
    cmp_end = lax.broadcasted_iota(jnp.int32, (1, n16), 1) * CMP_STRIDE + (CMP_LEN - 1)
    dist = tpos - cmp_end
    maskc = dist >= 0
    distf = dist.astype(F32)
    blk = lax.broadcasted_iota(jnp.int32, (1, nb), 1)
    cur = tpos // SEL_BLOCK
    forced = (blk == 0) | (blk == cur) | (blk == cur - 1)
    valid = blk * SEL_BLOCK <= tpos
    pool = pool_ref[...]
    for g in range(NSA_KV_HEADS):
        kc = kc_ref[0, 0, g]
        vc = vc_ref[0, 0, g]
        psum = jnp.zeros((tq, n16), F32)
        for hh in range(NSA_HG):
            h = g * NSA_HG + hh
            slope = 2.0 ** (-8.0 * (h + 1) / NSA_Q_HEADS)
            qh = q_ref[0, :, h * LANES:(h + 1) * LANES]
            sc = jnp.where(maskc, _dot_nt(qh, kc) * scale - slope * distf, NEG)
            e = jnp.exp(sc - jnp.max(sc, axis=-1, keepdims=True))
            p = jnp.where(maskc, e / jnp.sum(e, axis=-1, keepdims=True), 0.0)
            oc_ref[0, :, h * LANES:(h + 1) * LANES] = _dot(p.astype(BF16), vc).astype(oc_ref.dtype)
            psum = psum + p
        hi, lo = _split2(psum)
        imp = _dot(hi, pool) + _dot(lo, pool)
        val = jnp.where(valid, jnp.where(forced, BIG, imp), NEG)
        sel = jnp.zeros((tq, nb), F32)
        for _ in range(n_sel):
            mx = jnp.max(val, axis=-1, keepdims=True)
            first = jnp.min(jnp.where(val == mx, blk, nb), axis=-1, keepdims=True)
            hit = blk == first
            sel = jnp.where(hit, 1.0, sel)
            val = jnp.where(hit, -jnp.inf, val)
        sel_ref[0, g] = sel.astype(sel_ref.dtype)


def _nsa_cmp(proj, kvc, tq=256):
    b, s, _ = proj.shape
    n16 = s // CMP_STRIDE
    nb = s // SEL_BLOCK
    ratio = SEL_BLOCK // CMP_STRIDE
    n = lax.broadcasted_iota(jnp.int32, (n16, nb), 0)
    j = lax.broadcasted_iota(jnp.int32, (n16, nb), 1)
    pool = ((n >= ratio * j - 1) & (n <= ratio * j + ratio - 1) & (n < n16 - 1)).astype(BF16)
    wq = NSA_Q_HEADS * LANES
    kv_blk = lambda p: pl.BlockSpec((1, 1, NSA_KV_HEADS, n16, LANES), lambda i, t: (i, p, 0, 0, 0))
    return pl.pallas_call(
        functools.partial(_nsa_cmp_body, tq=tq, n16=n16, nb=nb, n_sel=min(N_SEL, nb)),
        grid=(b, s // tq),
        in_specs=[pl.BlockSpec((1, tq, wq), lambda i, t: (i, t, C_NQ // wq)),
                  kv_blk(0), kv_blk(1),
                  pl.BlockSpec((n16, nb), lambda i, t: (0, 0))],
        out_specs=[pl.BlockSpec((1, tq, wq), lambda i, t: (i, t, 0)),
                   pl.BlockSpec((1, NSA_KV_HEADS, tq, nb), lambda i, t: (i, 0, t, 0))],
        out_shape=[jax.ShapeDtypeStruct((b, s, wq), BF16),
                   jax.ShapeDtypeStruct((b, NSA_KV_HEADS, s, nb), BF16)],
        compiler_params=_params("parallel", "parallel"),
        name="nsa_cmp",
    )(proj, kvc, kvc, pool)


def _nsa_main_body(q_ref, ks_ref, vs_ref, kw_ref, vw_ref, sel_ref, oc_ref, gate_ref, o_ref, *, tq, tks, tkw, nb):
    g = pl.program_id(1)
    i = pl.program_id(2)
    hg = NSA_HG
    rows = hg * tq
    scale = NSA_HEAD_DIM ** -0.5
    q4 = jnp.concatenate([q_ref[0, :, h * LANES:(h + 1) * LANES] for h in range(hg)], axis=0)
    t1 = i * tq + lax.broadcasted_iota(jnp.int32, (tq, 1), 0)
    tpos = jnp.concatenate([t1] * hg, axis=0)
    slope0 = jnp.concatenate(
        [jnp.full((tq, 1), 2.0 ** (-8.0 * (h + 1) / NSA_Q_HEADS), F32) for h in range(hg)], axis=0)
    slope1 = jnp.concatenate(
        [jnp.full((tq, 1), 2.0 ** (-8.0 * (hg + h + 1) / NSA_Q_HEADS), F32) for h in range(hg)], axis=0)
    slope = jnp.where(g == 0, slope0, slope1)
    sel = sel_ref[0, 0]

    def flash(k_ref, v_ref, j_lo, j_hi, tk, mask_fn):
        def body(j, st):
            m, l, acc = st
            off = pl.multiple_of(j * tk, tk)
            k = k_ref[0, pl.ds(off, tk), :]
            v = v_ref[0, pl.ds(off, tk), :]
            kpos = j * tk + lax.broadcasted_iota(jnp.int32, (1, tk), 1)
            dist = tpos - kpos
            msk = mask_fn(j, dist)
            sc = jnp.where(msk, _dot_nt(q4, k) * scale - slope * dist.astype(F32), NEG)
            m_new = jnp.maximum(m, jnp.max(sc, axis=-1, keepdims=True))
            alpha = jnp.exp(m - m_new)
            p = jnp.where(msk, jnp.exp(sc - m_new), 0.0)
            l = alpha * l + jnp.sum(p, axis=-1, keepdims=True)
            acc = alpha * acc + _dot(p.astype(BF16), v)
            return m_new, l, acc

        init = (jnp.full((rows, 1), NEG, F32), jnp.zeros((rows, 1), F32), jnp.zeros((rows, LANES), F32))
        _, l, acc = lax.fori_loop(j_lo, j_hi, body, init)
        return acc / l

    def sel_mask(j, dist):
        per = tks // SEL_BLOCK
        bi = lax.broadcasted_iota(jnp.int32, (nb, tks), 0)
        ci = lax.broadcasted_iota(jnp.int32, (nb, tks), 1)
        expand = jnp.where(bi == j * per + ci // SEL_BLOCK, 1.0, 0.0).astype(BF16)
        hit = _dot(sel, expand) > 0.5
        return jnp.concatenate([hit] * hg, axis=0) & (dist >= 0)

    def win_mask(j, dist):
        return (dist >= 0) & (dist < WINDOW)

    o_sel = flash(ks_ref, vs_ref, 0, (i * tq + tq - 1) // tks + 1, tks, sel_mask)
    w_lo = jnp.maximum(i * tq - WINDOW, 0) // tkw
    o_win = flash(kw_ref, vw_ref, w_lo, (i * tq + tq - 1) // tkw + 1, tkw, win_mask)

    gates = _sigmoid(gate_ref[0])
    for h in range(hg):
        r0, r1 = h * tq, (h + 1) * tq
        oc = oc_ref[0, :, h * LANES:(h + 1) * LANES].astype(F32)
        out = (gates[:, h:h + 1] * oc + gates[:, hg + h:hg + h + 1] * o_sel[r0:r1]
               + gates[:, 2 * hg + h:2 * hg + h + 1] * o_win[r0:r1])
        o_ref[0, :, h * LANES:(h + 1) * LANES] = out.astype(o_ref.dtype)


def _nsa_main(proj, small, o_cmp, sel, tq=256, tks=512, tkw=256):
    b, s, _ = proj.shape
    nb = s // SEL_BLOCK
    wg = NSA_HG * LANES
    c_kv = C_NKV // LANES
    kv_spec = lambda part: pl.BlockSpec((1, s, LANES), lambda i, g, t: (i, 0, c_kv + 2 * part + g))
    return pl.pallas_call(
        functools.partial(_nsa_main_body, tq=tq, tks=tks, tkw=tkw, nb=nb),
        grid=(b, NSA_KV_HEADS, s // tq),
        in_specs=[pl.BlockSpec((1, tq, wg), lambda i, g, t: (i, t, C_NQ // wg + g)),
                  kv_spec(0), kv_spec(1), kv_spec(2), kv_spec(3),
                  pl.BlockSpec((1, 1, tq, nb), lambda i, g, t: (i, g, t, 0)),
                  pl.BlockSpec((1, tq, wg), lambda i, g, t: (i, t, g)),
                  pl.BlockSpec((1, tq, LANES), lambda i, g, t: (i, t, 1 + g))],
        out_specs=pl.BlockSpec((1, tq, wg), lambda i, g, t: (i, t, g)),
        out_shape=jax.ShapeDtypeStruct((b, s, NSA_Q_HEADS * LANES), BF16),
        compiler_params=_params("parallel", "parallel", "arbitrary"),
        name="nsa_main",
    )(proj, proj, proj, proj, proj, sel, o_cmp, small)


def _merge_body(ys_ref, yb_ref, yn_ref, mg_ref, x_ref, ws_ref, wb_ref, wn_ref, wo_ref, g_ref, o_ref):
    d = D_MODEL
    mg = mg_ref[...].astype(F32)
    mixed = (_sigmoid(mg[:, 0:d]) * _dot(ys_ref[...], ws_ref[...])
             + _sigmoid(mg[:, d:2 * d]) * _dot(yb_ref[...], wb_ref[...])
             + _sigmoid(mg[:, 2 * d:3 * d]) * _dot(yn_ref[...], wn_ref[...]))
    out = _dot(mixed.astype(BF16), wo_ref[...])
    r = lax.rsqrt(jnp.mean(out * out, axis=-1, keepdims=True) + NORM_EPS)
    o_ref[...] = x_ref[...] + out * r * g_ref[...]


def _merge(x2d, proj2d, y_ssm, y_sb, y_nsa, w_ssm, w_sb, w_nsa_pad, w_out, gain, tm=512):
    t, d = x2d.shape
    row = lambda w: pl.BlockSpec((tm, w), lambda i: (i, 0))
    full = lambda a: pl.BlockSpec(a.shape, lambda i: (0, 0))
    gain = gain.reshape(1, d).astype(F32)
    return pl.pallas_call(
        _merge_body,
        grid=(t // tm,),
        in_specs=[row(y_ssm.shape[1]), row(y_sb.shape[1]), row(y_nsa.shape[1]),
                  pl.BlockSpec((tm, 3 * d), lambda i: (i, C_MG // (3 * d))),
                  row(d), full(w_ssm), full(w_sb), full(w_nsa_pad), full(w_out), full(gain)],
        out_specs=row(d),
        out_shape=jax.ShapeDtypeStruct((t, d), F32),
        compiler_params=_params("parallel"),
        name="merge",
    )(y_ssm, y_sb, y_nsa, proj2d, x2d, w_ssm, w_sb, w_nsa_pad, w_out, gain)


def _ffn_conv_body(ug_ref, uv_ref, wg_ref, wv_ref, bg_ref, bv_ref, o_ref, gbuf, vbuf, *, tm):
    @pl.when(pl.program_id(2) == 0)
    def _():
        gbuf[0:SUBLANES, :] = jnp.zeros((SUBLANES, gbuf.shape[1]), F32)
        vbuf[0:SUBLANES, :] = jnp.zeros((SUBLANES, vbuf.shape[1]), F32)

    @pl.when(pl.program_id(2) > 0)
    def _():
        gbuf[0:SUBLANES, :] = gbuf[tm:tm + SUBLANES, :]
        vbuf[0:SUBLANES, :] = vbuf[tm:tm + SUBLANES, :]

    gbuf[SUBLANES:SUBLANES + tm, :] = ug_ref[0].astype(F32)
    vbuf[SUBLANES:SUBLANES + tm, :] = uv_ref[0].astype(F32)
    cg = bg_ref[...]
    cv = bv_ref[...]
    for k in range(FFN_CONV):
        off = SUBLANES - (FFN_CONV - 1) + k
        cg = cg + wg_ref[k:k + 1, :] * gbuf[off:off + tm, :]
        cv = cv + wv_ref[k:k + 1, :] * vbuf[off:off + tm, :]
    gelu = 0.5 * cg * (1.0 + jnp.tanh(0.7978845608028654 * (cg + 0.044715 * cg * cg * cg)))
    o_ref[0] = (gelu * cv).astype(o_ref.dtype)


def _ffn_conv(u, conv_w, conv_b, tm=512, tc=1408):
    b, s, _ = u.shape
    nct = D_FF // tc
    cw = conv_w.astype(F32)
    cbias = conv_b.reshape(1, -1).astype(F32)
    return pl.pallas_call(
        functools.partial(_ffn_conv_body, tm=tm),
        grid=(b, nct, s // tm),
        in_specs=[pl.BlockSpec((1, tm, tc), lambda i, c, t: (i, t, c)),
                  pl.BlockSpec((1, tm, tc), lambda i, c, t: (i, t, nct + c)),
                  pl.BlockSpec((FFN_CONV, tc), lambda i, c, t: (0, c)),
                  pl.BlockSpec((FFN_CONV, tc), lambda i, c, t: (0, nct + c)),
                  pl.BlockSpec((1, tc), lambda i, c, t: (0, c)),
                  pl.BlockSpec((1, tc), lambda i, c, t: (0, nct + c))],
        out_specs=pl.BlockSpec((1, tm, tc), lambda i, c, t: (i, t, c)),
        out_shape=jax.ShapeDtypeStruct((b, s, D_FF), BF16),
        scratch_shapes=[pltpu.VMEM((SUBLANES + tm, tc), F32), pltpu.VMEM((SUBLANES + tm, tc), F32)],
        compiler_params=_params("parallel", "parallel", "arbitrary"),
        name="ffn_conv",
    )(u, u, cw, cw, cbias, cbias)


def _ffn_down_body(a_ref, w_ref, x_ref, g_ref, o_ref):
    f = _dot(a_ref[...], w_ref[...])
    r = lax.rsqrt(jnp.mean(f * f, axis=-1, keepdims=True) + NORM_EPS)
    o_ref[...] = x_ref[...] + f * r * g_ref[...]


def _ffn_down(act2d, w_down, x2d, gain, tm=512):
    t, d = x2d.shape
    k = act2d.shape[1]
    return pl.pallas_call(
        _ffn_down_body,
        grid=(t // tm,),
        in_specs=[pl.BlockSpec((tm, k), lambda i: (i, 0)),
                  pl.BlockSpec((k, d), lambda i: (0, 0)),
                  pl.BlockSpec((tm, d), lambda i: (i, 0)),
                  pl.BlockSpec((1, d), lambda i: (0, 0))],
        out_specs=pl.BlockSpec((tm, d), lambda i: (i, 0)),
        out_shape=jax.ShapeDtypeStruct((t, d), F32),
        compiler_params=_params("parallel"),
        name="ffn_down",
    )(act2d, w_down, x2d, gain.reshape(1, d).astype(F32))


def _pack_w_in(w):
    d = w.shape[0]
    o = 0
    z = w[:, o:o + 1024]; o += 1024
    xs = w[:, o:o + 1024]; o += 1024
    bc = w[:, o:o + 1024]; o += 1024
    dt = w[:, o:o + SSM_HEADS]; o += SSM_HEADS
    sb = w[:, o:o + 1536]; o += 1536
    nq = w[:, o:o + 512]; o += 512
    nkv = w[:, o:o + 768]; o += 768
    ng = w[:, o:o + 24]; o += 24
    mg = w[:, o:o + 3072]
    hd = NSA_HEAD_DIM
    pad64 = lambda a: jnp.pad(a.reshape(d, -1, hd), ((0, 0), (0, 0), (0, LANES - hd))).reshape(d, -1)
    nq_pad = pad64(nq)
    kv_pad = pad64(nkv[:, 2 * NSA_KV_HEADS * hd:])
    cmp_raw = nkv[:, :2 * NSA_KV_HEADS * hd]
    main = jnp.concatenate([z, xs, bc, mg, nq_pad, kv_pad, sb, cmp_raw], axis=1)
    ng3 = ng.reshape(d, 3, NSA_KV_HEADS, NSA_HG)
    gate_blocks = [jnp.pad(ng3[:, :, g, :].reshape(d, 3 * NSA_HG), ((0, 0), (0, LANES - 3 * NSA_HG)))
                   for g in range(NSA_KV_HEADS)]
    small = jnp.concatenate([jnp.pad(dt, ((0, 0), (0, LANES - SSM_HEADS)))] + gate_blocks, axis=1)
    return main.astype(BF16), small.astype(BF16)


def kernel(x, pre_mix_norm, w_in, ssm_conv_w, ssm_conv_b, ssm_dt_bias, ssm_a_log, ssm_d, ssm_norm, cmp_pos_k, cmp_w1_k, cmp_w2_k, cmp_pos_v, cmp_w1_v, cmp_w2_v, w_br_ssm, w_br_sb, w_br_nsa, w_out, post_mix_norm, pre_ffn_norm, ffn_w_up, ffn_conv_w, ffn_conv_b, ffn_w_down, post_ffn_norm):
    b, s, d = x.shape
    t = b * s
    x2d = x.reshape(t, d)
    hd = NSA_HEAD_DIM
    for l in range(w_in.shape[0]):
        w_main, w_small = _pack_w_in(w_in[l])
        proj = _norm_matmul(x2d, pre_mix_norm[l], w_main, BF16, tm=1024, tn=768).reshape(b, s, N_MAIN)
        small = _norm_matmul(x2d, pre_mix_norm[l], w_small, F32, tm=1024, tn=N_SMALL).reshape(b, s, N_SMALL)
        y_ssm = _ssd(proj, small, ssm_conv_w[l], ssm_conv_b[l], ssm_dt_bias[l], ssm_a_log[l], ssm_d[l], ssm_norm[l])
        y_sb = _sb_attention(proj)
        kvc = _nsa_compress(proj, cmp_pos_k[l], cmp_w1_k[l], cmp_w2_k[l], cmp_pos_v[l], cmp_w1_v[l], cmp_w2_v[l])
        o_cmp, sel = _nsa_cmp(proj, kvc)
        y_nsa = _nsa_main(proj, small, o_cmp, sel)
        w_nsa_pad = jnp.pad(w_br_nsa[l].reshape(NSA_Q_HEADS, hd, d), ((0, 0), (0, LANES - hd), (0, 0)))
        x2d = _merge(x2d, proj.reshape(t, N_MAIN), y_ssm.reshape(t, -1), y_sb.reshape(t, -1), y_nsa.reshape(t, -1),
                     w_br_ssm[l].astype(BF16), w_br_sb[l].astype(BF16),
                     w_nsa_pad.reshape(NSA_Q_HEADS * LANES, d).astype(BF16), w_out[l].astype(BF16), post_mix_norm[l])
        u = _norm_matmul(x2d, pre_ffn_norm[l], ffn_w_up[l].astype(BF16), BF16, tm=1024, tn=512)
        act = _ffn_conv(u.reshape(b, s, 2 * D_FF), ffn_conv_w[l], ffn_conv_b[l])
        x2d = _ffn_down(act.reshape(t, D_FF), ffn_w_down[l].astype(BF16), x2d, post_ffn_norm[l])
    return x2d.reshape(b, s, d)
```

```python
import functools

import jax
import jax.numpy as jnp
from jax import lax
from jax.experimental import pallas as pl
from jax.experimental.pallas import tpu as pltpu

F32 = jnp.float32
BF16 = jnp.bfloat16

D_MODEL = 1024
SSM_D_INNER = 1024
SSM_HEAD_DIM = 64
SSM_HEADS = 16
SSM_GROUPS = 4
SSM_D_STATE = 128
SSM_CONV = 4
SSM_CHUNK = 128
SB_HEAD_DIM = 128
SB_HEADS = 4
NSA_HEAD_DIM = 64
NSA_Q_HEADS = 8
NSA_KV_HEADS = 2
NSA_HG = NSA_Q_HEADS // NSA_KV_HEADS
CMP_STRIDE = 16
CMP_LEN = 32
SEL_BLOCK = 64
N_SEL = 8
WINDOW = 512
D_FF = 2816
FFN_CONV = 3
NORM_EPS = 1e-6
NEG = -1e30
BIG = 1e30

LANES = 128
SUBLANES = 8
VMEM_LIMIT = 56 * 1024 * 1024

SB_CUTOFF = 104.0
LOG2E = 1.4426950408889634
NSA_TILE = 256

C_Z, C_XS, C_BC, C_MG, C_NQ, C_NKV, C_SB, C_CMP = 0, 1024, 2048, 3072, 6144, 7168, 8192, 9728
N_MAIN = 9984
N_SMALL = 384


def _dot(a, b):
    return jnp.dot(a, b, preferred_element_type=F32)


def _dot_nt(a, b):
    return lax.dot_general(a, b, (((1,), (1,)), ((), ())), preferred_element_type=F32)


def _split2(x):
    hi = x.astype(BF16)
    lo = (x - hi.astype(F32)).astype(BF16)
    return hi, lo


def _split3(x):
    hi = x.astype(BF16)
    r = x - hi.astype(F32)
    mid = r.astype(BF16)
    lo = (r - mid.astype(F32)).astype(BF16)
    return hi, mid, lo


def _sigmoid(x):
    return 1.0 / (1.0 + jnp.exp(-x))


def _softplus(x):
    return jnp.maximum(x, 0.0) + jnp.log1p(jnp.exp(-jnp.abs(x)))


def _params(*sem):
    return pltpu.CompilerParams(dimension_semantics=sem, vmem_limit_bytes=VMEM_LIMIT)


def _norm_matmul_body(x_ref, g_ref, w_ref, o_ref, h_ref):
    @pl.when(pl.program_id(1) == 0)
    def _():
        x = x_ref[...]
        r = lax.rsqrt(jnp.mean(x * x, axis=-1, keepdims=True) + NORM_EPS)
        h_ref[...] = (x * r * g_ref[...]).astype(BF16)

    o_ref[...] = _dot(h_ref[...], w_ref[...]).astype(o_ref.dtype)


def _norm_matmul(x2d, gain, w, out_dtype, tm, tn):
    t, d = x2d.shape
    n = w.shape[1]
    return pl.pallas_call(
        _norm_matmul_body,
        grid=(t // tm, n // tn),
        in_specs=[pl.BlockSpec((tm, d), lambda i, j: (i, 0)),
                  pl.BlockSpec((1, d), lambda i, j: (0, 0)),
                  pl.BlockSpec((d, tn), lambda i, j: (0, j))],
        out_specs=pl.BlockSpec((tm, tn), lambda i, j: (i, j)),
        out_shape=jax.ShapeDtypeStruct((t, n), out_dtype),
        scratch_shapes=[pltpu.VMEM((tm, d), BF16)],
        compiler_params=_params("parallel", "arbitrary"),
        name="norm_matmul",
    )(x2d, gain.reshape(1, d).astype(F32), w)


def _ssd_body(z_ref, xs_ref, bc_ref, sm_ref, cw_ref, cb_ref, dtb_ref, alog_ref, dexp_ref, nw_ref,
              o_ref, xbuf, state, ybuf):
    q = SSM_CHUNK
    c = pl.program_id(1)

    @pl.when(c == 0)
    def _():
        xbuf[0:SUBLANES, :] = jnp.zeros((SUBLANES, 2 * SSM_D_INNER), F32)
        state[...] = jnp.zeros_like(state)

    @pl.when(c > 0)
    def _():
        xbuf[0:SUBLANES, :] = xbuf[q:q + SUBLANES, :]

    xbuf[SUBLANES:SUBLANES + q, 0:SSM_D_INNER] = xs_ref[0].astype(F32)
    xbuf[SUBLANES:SUBLANES + q, SSM_D_INNER:] = bc_ref[0].astype(F32)
    xin = xbuf[...]
    conv = cb_ref[...] + cw_ref[SSM_CONV - 1:SSM_CONV, :] * xin[SUBLANES:]
    for k in range(SSM_CONV - 1):
        conv = conv + cw_ref[k:k + 1, :] * pltpu.roll(xin, SSM_CONV - 1 - k, axis=0)[SUBLANES:]
    xbc = conv * _sigmoid(conv)
    xs = xbc[:, :SSM_D_INNER]
    bm = xbc[:, SSM_D_INNER:SSM_D_INNER + SSM_GROUPS * SSM_D_STATE].astype(BF16)
    cm = xbc[:, SSM_D_INNER + SSM_GROUPS * SSM_D_STATE:].astype(BF16)

    dt = _softplus(sm_ref[0] + dtb_ref[...])
    a = -jnp.exp(alog_ref[...])
    dta = dt * a
    row = lax.broadcasted_iota(jnp.int32, (q, q), 0)
    col = lax.broadcasted_iota(jnp.int32, (q, q), 1)
    tri = row >= col
    tril = jnp.where(tri, 1.0, 0.0).astype(BF16)
    d1, d2, d3 = _split3(dta)
    acum = _dot(tril, d1) + _dot(tril, d2) + _dot(tril, d3)
    acum_t = acum.T
    dt_t = dt.T
    exp_acum = jnp.exp(acum)
    exp_last = exp_acum[q - 1:q, :]
    to_end_t = jnp.exp(acum_t[:, q - 1:q] - acum_t) * dt_t
    xs_t = xs.T
    xs_b = xs.astype(BF16)

    for g in range(SSM_GROUPS):
        bg = bm[:, g * SSM_D_STATE:(g + 1) * SSM_D_STATE]
        cg = cm[:, g * SSM_D_STATE:(g + 1) * SSM_D_STATE]
        cb = _dot_nt(cg, bg)
        for hh in range(SSM_HEADS // SSM_GROUPS):
            h = g * (SSM_HEADS // SSM_GROUPS) + hh
            lo, hi = h * SSM_HEAD_DIM, (h + 1) * SSM_HEAD_DIM
            seg = acum[:, h:h + 1] - acum_t[h:h + 1, :]
            decay = jnp.exp(jnp.where(tri, seg, NEG))
            w = (cb * decay * dt_t[h:h + 1, :]).astype(BF16)
            st = state[h]
            y = _dot(w, xs_b[:, lo:hi])
            y = y + _dot_nt(cg, st.astype(BF16)) * exp_acum[:, h:h + 1]
            ybuf[:, lo:hi] = y
            xw = (xs_t[lo:hi, :] * to_end_t[h:h + 1, :]).astype(BF16)
            state[h] = st * exp_last[:, h:h + 1] + _dot(xw, bg)

    y = ybuf[...] + dexp_ref[...] * xs
    z = z_ref[0].astype(F32)
    y = y * (z * _sigmoid(z))
    gw = SSM_D_INNER // SSM_GROUPS
    for g in range(SSM_GROUPS):
        yg = y[:, g * gw:(g + 1) * gw]
        r = lax.rsqrt(jnp.mean(yg * yg, axis=-1, keepdims=True) + NORM_EPS)
        o_ref[0, :, g * gw:(g + 1) * gw] = (yg * r * nw_ref[:, g * gw:(g + 1) * gw]).astype(o_ref.dtype)


def _ssd(proj, small, conv_w, conv_b, dt_bias, a_log, d_skip, norm_w):
    b, s, _ = proj.shape
    q = SSM_CHUNK
    wd = SSM_D_INNER
    pad = LANES - SSM_HEADS
    dtb = jnp.pad(dt_bias.astype(F32), (0, pad)).reshape(1, LANES)
    alog = jnp.pad(a_log.astype(F32), (0, pad)).reshape(1, LANES)
    dexp = jnp.repeat(d_skip.astype(F32), SSM_HEAD_DIM).reshape(1, wd)
    const = lambda shape: pl.BlockSpec(shape, lambda i, j: (0, 0))
    return pl.pallas_call(
        _ssd_body,
        grid=(b, s // q),
        in_specs=[pl.BlockSpec((1, q, wd), lambda i, j: (i, j, C_Z // wd)),
                  pl.BlockSpec((1, q, wd), lambda i, j: (i, j, C_XS // wd)),
                  pl.BlockSpec((1, q, wd), lambda i, j: (i, j, C_BC // wd)),
                  pl.BlockSpec((1, q, LANES), lambda i, j: (i, j, 0)),
                  const((SSM_CONV, 2 * wd)), const((1, 2 * wd)), const((1, LANES)), const((1, LANES)),
                  const((1, wd)), const((1, wd))],
        out_specs=pl.BlockSpec((1, q, wd), lambda i, j: (i, j, 0)),
        out_shape=jax.ShapeDtypeStruct((b, s, wd), BF16),
        scratch_shapes=[pltpu.VMEM((SUBLANES + q, 2 * wd), F32),
                        pltpu.VMEM((SSM_HEADS, SSM_HEAD_DIM, SSM_D_STATE), F32),
                        pltpu.VMEM((q, wd), F32)],
        compiler_params=_params("parallel", "arbitrary"),
        name="ssd",
    )(proj, proj, proj, small, conv_w.astype(F32), conv_b.reshape(1, -1).astype(F32), dtb, alog, dexp,
      norm_w.reshape(1, wd).astype(F32))


def _sb_body(q_ref, k_ref, v_ref, u_ref, o_ref, *, tq, tk):
    i = pl.program_id(1)
    nh, hd = SB_HEADS, SB_HEAD_DIM
    rows = nh * tq
    r1 = lax.broadcasted_iota(jnp.int32, (tq, tk), 0)
    c1 = lax.broadcasted_iota(jnp.int32, (tq, tk), 1)
    diag_bias = jnp.concatenate([jnp.where(c1 < r1, 0.0, NEG)] * nh, axis=0)
    u = u_ref[...]
    qs = [q_ref[0, :, h * hd:(h + 1) * hd] for h in range(nh)]

    def tile(j, carry, acc, bias):
        off = pl.multiple_of(j * tk, tk)
        z = jnp.concatenate(
            [_dot_nt(qs[h], k_ref[0, pl.ds(off, tk), h * hd:(h + 1) * hd]) for h in range(nh)], axis=0)
        if bias is not None:
            z = z + bias
        sp = jnp.maximum(z, 0.0) + jnp.log(1.0 + jnp.exp(-jnp.abs(z)))
        hi, lo = _split2(sp)
        later = carry - (_dot(hi, u) + _dot(lo, u))
        w = jnp.exp(z - sp + later).astype(BF16)
        acc = acc + jnp.concatenate(
            [_dot(w[h * tq:(h + 1) * tq], v_ref[0, pl.ds(off, tk), h * hd:(h + 1) * hd]) for h in range(nh)], axis=0)
        carry = later[:, 0:1] - sp[:, 0:1]
        return carry, acc

    def cond(st):
        j, carry, _ = st
        return jnp.logical_and(j >= 0, jnp.max(carry) > -SB_CUTOFF)

    def body(st):
        j, carry, acc = st
        carry, acc = tile(j, carry, acc, None)
        return j - 1, carry, acc

    carry, acc = tile(i, jnp.zeros((rows, 1), F32), jnp.zeros((rows, hd), F32), diag_bias)
    _, _, acc = lax.while_loop(cond, body, (i - 1, carry, acc))
    for h in range(nh):
        o_ref[0, :, h * hd:(h + 1) * hd] = acc[h * tq:(h + 1) * tq].astype(o_ref.dtype)


def _sb_attention(proj, tq=256, tk=256):
    assert tq == tk, "the diagonal key tile is assumed to coincide with the query tile"
    b, s, _ = proj.shape
    wd = SB_HEADS * SB_HEAD_DIM
    c0 = C_SB // wd
    r = lax.broadcasted_iota(jnp.int32, (tk, tk), 0)
    c = lax.broadcasted_iota(jnp.int32, (tk, tk), 1)
    u = (r > c).astype(BF16)
    return pl.pallas_call(
        functools.partial(_sb_body, tq=tq, tk=tk),
        grid=(b, s // tq),
        in_specs=[pl.BlockSpec((1, tq, wd), lambda i, j: (i, j, c0)),
                  pl.BlockSpec((1, s, wd), lambda i, j: (i, 0, c0 + 1)),
                  pl.BlockSpec((1, s, wd), lambda i, j: (i, 0, c0 + 2)),
                  pl.BlockSpec((tk, tk), lambda i, j: (0, 0))],
        out_specs=pl.BlockSpec((1, tq, wd), lambda i, j: (i, j, 0)),
        out_shape=jax.ShapeDtypeStruct((b, s, wd), BF16),
        compiler_params=_params("parallel", "arbitrary"),
        name="sb",
    )(proj, proj, proj, u)


def _nsa_compress_body(a_ref, an_ref, w1_ref, pos_ref, w2_ref, o_ref):
    half = CMP_STRIDE * NSA_HEAD_DIM
    w1 = w1_ref[0]
    pos = pos_ref[0]
    for g in range(NSA_KV_HEADS):
        a = a_ref[0, 0, g].astype(F32)
        an = an_ref[0, 0, g].astype(F32)
        pre = (_dot((a + pos[:, :half]).astype(BF16), w1[:half])
               + _dot((an + pos[:, half:]).astype(BF16), w1[half:]))
        act = (pre * _sigmoid(pre)).astype(BF16)
        o_ref[0, 0, g] = _dot(act, w2_ref[0]).astype(o_ref.dtype)


def _nsa_compress(proj, pos_k, w1_k, w2_k, pos_v, w1_v, w2_v):
    b, s, _ = proj.shape
    n16 = s // CMP_STRIDE
    hd = NSA_HEAD_DIM
    half = CMP_STRIDE * hd
    raw = proj[:, :, C_CMP:C_CMP + 2 * NSA_KV_HEADS * hd]
    a = raw.reshape(b, n16, CMP_STRIDE, 2, NSA_KV_HEADS, hd).transpose(0, 3, 4, 1, 2, 5)
    a = a.reshape(b, 2, NSA_KV_HEADS, n16, half)
    a_next = jnp.roll(a, -1, axis=3)
    w1 = jnp.stack([w1_k, w1_v]).astype(BF16)
    pos = jnp.stack([pos_k, pos_v]).astype(F32).reshape(2, 1, CMP_LEN * hd)
    w2 = jnp.pad(jnp.stack([w2_k, w2_v]), ((0, 0), (0, 0), (0, LANES - hd))).astype(BF16)
    blk = pl.BlockSpec((1, 1, NSA_KV_HEADS, n16, half), lambda i, p: (i, p, 0, 0, 0))
    return pl.pallas_call(
        _nsa_compress_body,
        grid=(b, 2),
        in_specs=[blk, blk,
                  pl.BlockSpec((1, CMP_LEN * hd, hd), lambda i, p: (p, 0, 0)),
                  pl.BlockSpec((1, 1, CMP_LEN * hd), lambda i, p: (p, 0, 0)),
                  pl.BlockSpec((1, hd, LANES), lambda i, p: (p, 0, 0))],
        out_specs=pl.BlockSpec((1, 1, NSA_KV_HEADS, n16, LANES), lambda i, p: (i, p, 0, 0, 0)),
        out_shape=jax.ShapeDtypeStruct((b, 2, NSA_KV_HEADS, n16, LANES), BF16),
        compiler_params=_params("parallel", "parallel"),
        name="nsa_compress",
    )(a, a_next, w1, pos, w2)


def _nsa_cmp_body(q_ref, kc_ref, vct_ref, qfeat_ref, cfeat_ref, poolt_ref, tpool_ref, oc_ref, sel_ref, flag_ref,
                  *, tq, n16, nb, n_sel):
    i = pl.program_id(1)
    tpos = i * tq + lax.broadcasted_iota(jnp.int32, (1, tq), 1)
    cmp_end = lax.broadcasted_iota(jnp.int32, (n16, 1), 0) * CMP_STRIDE + (CMP_LEN - 1)
    bias = jnp.where(cmp_end <= tpos, 0.0, NEG)
    has_past = jnp.where(tpos >= CMP_LEN - 1, 1.0, 0.0)
    bias4 = jnp.concatenate([bias] * NSA_HG, axis=1)
    has_past4 = jnp.concatenate([has_past] * NSA_HG, axis=1)
    blk = lax.broadcasted_iota(jnp.int32, (nb, 1), 0)
    blkf = blk.astype(F32)
    cur = tpos // SEL_BLOCK
    forced = (blk == 0) | (blk == cur) | (blk == cur - 1)
    valid = blk * SEL_BLOCK <= tpos
    poolt = poolt_ref[...]
    for g in range(NSA_KV_HEADS):
        kca = kc_ref[0, 0, g] + cfeat_ref[...]
        vct = vct_ref[0, g]
        heads = [g * NSA_HG + hh for hh in range(NSA_HG)]
        qa = jnp.concatenate(
            [q_ref[0, :, h * LANES:(h + 1) * LANES] + qfeat_ref[g, h - heads[0], 0:1, :] for h in heads], axis=0)
        sc = _dot_nt(kca, qa) + bias4
        e = jnp.exp2(sc - jnp.max(sc, axis=0, keepdims=True))
        p = e * (has_past4 / jnp.sum(e, axis=0, keepdims=True))
        oct = _dot(vct, p.astype(BF16)).astype(oc_ref.dtype)
        psum = jnp.zeros((n16, tq), F32)
        for hh, h in enumerate(heads):
            oc_ref[0, h] = oct[:, hh * tq:(hh + 1) * tq]
            psum = psum + p[:, hh * tq:(hh + 1) * tq]
        hi, lo = _split2(psum)
        imp = _dot(poolt, hi) + _dot(poolt, lo)
        val = jnp.where(valid, jnp.where(forced, BIG, imp), NEG)
        sel = jnp.zeros((nb, tq), F32)
        for _ in range(n_sel):
            mx = jnp.max(val, axis=0, keepdims=True)
            first = jnp.min(jnp.where(val == mx, blkf, float(nb)), axis=0, keepdims=True)
            hit = blkf == first
            sel = jnp.where(hit, 1.0, sel)
            val = jnp.where(hit, -jnp.inf, val)
        sel = jnp.where(valid, sel, 0.0).T
        sel_ref[0, g] = sel.astype(sel_ref.dtype)
        any_row = jnp.max(sel, axis=0, keepdims=True).astype(BF16)
        flag_ref[0, g, 0] = (_dot(any_row, tpool_ref[...]) > 0.5).astype(jnp.int32)


def _nsa_cmp(proj, kvc, tq, tk):
    b, s, _ = proj.shape
    n16 = s // CMP_STRIDE
    nb = s // SEL_BLOCK
    ratio = SEL_BLOCK // CMP_STRIDE
    n = lax.broadcasted_iota(jnp.int32, (nb, n16), 1)
    j = lax.broadcasted_iota(jnp.int32, (nb, n16), 0)
    poolt = ((n >= ratio * j - 1) & (n <= ratio * j + ratio - 1) & (n < n16 - 1)).astype(BF16)
    bi = lax.broadcasted_iota(jnp.int32, (nb, LANES), 0)
    ti = lax.broadcasted_iota(jnp.int32, (nb, LANES), 1)
    tpool = (bi // (tk // SEL_BLOCK) == ti).astype(BF16)
    qfeat, cfeat = _alibi_features(jnp.arange(n16, dtype=jnp.int32) * CMP_STRIDE + (CMP_LEN - 1))
    vct = jnp.swapaxes(kvc[:, 1], -1, -2)
    wq = NSA_Q_HEADS * LANES
    nq = s // tq
    const = lambda a: pl.BlockSpec(a.shape, lambda i, t: (0,) * a.ndim)
    return pl.pallas_call(
        functools.partial(_nsa_cmp_body, tq=tq, n16=n16, nb=nb, n_sel=min(N_SEL, nb)),
        grid=(b, nq),
        in_specs=[pl.BlockSpec((1, tq, wq), lambda i, t: (i, t, C_NQ // wq)),
                  pl.BlockSpec((1, 1, NSA_KV_HEADS, n16, LANES), lambda i, t: (i, 0, 0, 0, 0)),
                  pl.BlockSpec((1, NSA_KV_HEADS, LANES, n16), lambda i, t: (i, 0, 0, 0)),
                  const(qfeat), const(cfeat), const(poolt), const(tpool)],
        out_specs=[pl.BlockSpec((1, NSA_Q_HEADS, LANES, tq), lambda i, t: (i, 0, 0, t)),
                   pl.BlockSpec((1, NSA_KV_HEADS, tq, nb), lambda i, t: (i, 0, t, 0)),
                   pl.BlockSpec((1, NSA_KV_HEADS, 1, 1, LANES), lambda i, t: (i, 0, t, 0, 0))],
        out_shape=[jax.ShapeDtypeStruct((b, NSA_Q_HEADS, LANES, s), BF16),
                   jax.ShapeDtypeStruct((b, NSA_KV_HEADS, s, nb), BF16),
                   jax.ShapeDtypeStruct((b, NSA_KV_HEADS, nq, 1, LANES), jnp.int32)],
        compiler_params=_params("parallel", "parallel"),
        name="nsa_cmp",
    )(proj, kvc, vct, qfeat, cfeat, poolt, tpool)


def _nsa_main_body(flags_ref, q_ref, ks_ref, vs_ref, kw_ref, vw_ref, sel_ref, oc_ref, gate_ref,
                   qfeat_ref, kfeat_ref, onehot_ref, vone_ref, o_ref, m_scr, acc_scr, *, t, nq):
    bi = pl.program_id(0)
    g = pl.program_id(1)
    i = pl.program_id(2)
    hg = NSA_HG
    hd = NSA_HEAD_DIM
    rows = hg * t
    fbase = ((bi * NSA_KV_HEADS + g) * nq + i) * nq
    qa = jnp.concatenate(
        [q_ref[0, :, h * LANES:(h + 1) * LANES] + qfeat_ref[0, h, 0:1, :] for h in range(hg)], axis=0)
    selb = ((sel_ref[0, 0].astype(F32) - 1.0) * BIG).astype(BF16)
    qs = jnp.concatenate([qa, jnp.concatenate([selb] * hg, axis=0)], axis=1)
    vone = vone_ref[...]
    r = lax.broadcasted_iota(jnp.int32, (t, t), 0)
    c = lax.broadcasted_iota(jnp.int32, (t, t), 1)
    causal = jnp.concatenate([jnp.where(c <= r, 0.0, NEG)] * hg, axis=0)
    lower = jnp.concatenate([jnp.where(c > r, 0.0, NEG)] * hg, axis=0)

    def reset():
        m_scr[...] = jnp.full((rows, LANES), NEG, F32)
        acc_scr[...] = jnp.zeros((rows, LANES), F32)

    def step(qmat, ka, va, bias):
        sc = _dot_nt(qmat, ka)
        if bias is not None:
            sc = sc + bias
        m_old = m_scr[...]
        m_new = jnp.maximum(m_old, jnp.max(sc, axis=-1, keepdims=True))
        p = jnp.exp2(sc - jnp.concatenate([m_new] * (t // LANES), axis=1)).astype(BF16)
        acc_scr[...] = jnp.exp2(m_old - m_new) * acc_scr[...] + _dot(p, va)
        m_scr[...] = m_new

    def result():
        acc = acc_scr[...]
        return acc / acc[:, hd:hd + 1]

    def ktile(ref, j):
        off = pl.multiple_of(j * t, t)
        return ref[0, pl.ds(off, t), :] + kfeat_ref[pl.ds(off, t), :]

    def vtile(ref, j):
        off = pl.multiple_of(j * t, t)
        return ref[0, pl.ds(off, t), :] + vone

    def ksel(j):
        off = pl.multiple_of(j * t, t)
        return jnp.concatenate([ktile(ks_ref, j), onehot_ref[pl.ds(off, t), :]], axis=1)

    reset()

    def sel_body(j, carry):
        @pl.when(flags_ref[fbase + j] > 0)
        def _():
            step(qs, ksel(j), vtile(vs_ref, j), None)
        return carry

    lax.fori_loop(0, i, sel_body, 0)
    step(qs, ksel(i), vtile(vs_ref, i), causal)
    o_sel = result()

    reset()

    @pl.when(i >= 2)
    def _():
        step(qa, ktile(kw_ref, i - 2), vtile(vw_ref, i - 2), lower)

    @pl.when(i >= 1)
    def _():
        step(qa, ktile(kw_ref, i - 1), vtile(vw_ref, i - 1), None)

    step(qa, ktile(kw_ref, i), vtile(vw_ref, i), causal)
    o_win = result()

    gates = _sigmoid(gate_ref[0])
    for h in range(hg):
        r0, r1 = h * t, (h + 1) * t
        oc = oc_ref[0, h].astype(F32).T
        out = (gates[:, h:h + 1] * oc + gates[:, hg + h:hg + h + 1] * o_sel[r0:r1]
               + gates[:, 2 * hg + h:2 * hg + h + 1] * o_win[r0:r1])
        o_ref[0, :, h * LANES:(h + 1) * LANES] = out.astype(o_ref.dtype)


def _alibi_features(pos):
    hd = NSA_HEAD_DIM
    kp = [(pos >> 8) << 8, ((pos >> 4) & 15) << 4, pos & 15]
    kfeat = jnp.zeros((pos.shape[0], LANES), F32)
    qfeat = jnp.zeros((NSA_Q_HEADS, LANES), F32)
    slope2 = jnp.asarray([LOG2E * 2.0 ** (-8.0 * (h + 1) / NSA_Q_HEADS) for h in range(NSA_Q_HEADS)], F32)
    qp = [p.astype(F32) for p in _split3(slope2)]
    for a in range(3):
        for bb in range(3):
            lane = hd + 3 * a + bb
            kfeat = kfeat.at[:, lane].set(kp[a].astype(F32))
            qfeat = qfeat.at[:, lane].set(qp[bb])
    qfeat = jnp.broadcast_to(qfeat.reshape(NSA_KV_HEADS, NSA_HG, 1, LANES), (NSA_KV_HEADS, NSA_HG, SUBLANES, LANES))
    return qfeat.astype(BF16), kfeat.astype(BF16)


def _nsa_features(s):
    nb = s // SEL_BLOCK
    pos = jnp.arange(s, dtype=jnp.int32)
    qfeat, kfeat = _alibi_features(pos)
    onehot = (pos[:, None] // SEL_BLOCK == jnp.arange(nb, dtype=jnp.int32)[None, :])
    vone = jnp.zeros((1, LANES), F32).at[0, NSA_HEAD_DIM].set(1.0)
    return qfeat, kfeat, onehot.astype(BF16), vone.astype(BF16)


def _nsa_main(proj, small, o_cmp, sel, flags, t):
    b, s, _ = proj.shape
    nb = s // SEL_BLOCK
    nq = s // t
    hg = NSA_HG
    wg = hg * LANES
    c_kv = C_NKV // LANES
    qfeat, kfeat, onehot, vone = _nsa_features(s)
    kv_spec = lambda part: pl.BlockSpec((1, s, LANES), lambda i, g, j, f: (i, 0, c_kv + 2 * part + g))
    grid_spec = pltpu.PrefetchScalarGridSpec(
        num_scalar_prefetch=1,
        grid=(b, NSA_KV_HEADS, nq),
        in_specs=[pl.BlockSpec((1, t, wg), lambda i, g, j, f: (i, j, C_NQ // wg + g)),
                  kv_spec(0), kv_spec(1), kv_spec(2), kv_spec(3),
                  pl.BlockSpec((1, 1, t, nb), lambda i, g, j, f: (i, g, j, 0)),
                  pl.BlockSpec((1, hg, LANES, t), lambda i, g, j, f: (i, g, 0, j)),
                  pl.BlockSpec((1, t, LANES), lambda i, g, j, f: (i, j, 1 + g)),
                  pl.BlockSpec((1, hg, SUBLANES, LANES), lambda i, g, j, f: (g, 0, 0, 0)),
                  pl.BlockSpec((s, LANES), lambda i, g, j, f: (0, 0)),
                  pl.BlockSpec((s, nb), lambda i, g, j, f: (0, 0)),
                  pl.BlockSpec((1, LANES), lambda i, g, j, f: (0, 0))],
        out_specs=pl.BlockSpec((1, t, wg), lambda i, g, j, f: (i, j, g)),
        scratch_shapes=[pltpu.VMEM((hg * t, LANES), F32), pltpu.VMEM((hg * t, LANES), F32)])
    flat = flags[:, :, :, 0, :nq].reshape(-1)
    return pl.pallas_call(
        functools.partial(_nsa_main_body, t=t, nq=nq),
        grid_spec=grid_spec,
        out_shape=jax.ShapeDtypeStruct((b, s, NSA_Q_HEADS * LANES), BF16),
        compiler_params=_params("parallel", "parallel", "arbitrary"),
        name="nsa_main",
    )(flat, proj, proj, proj, proj, proj, sel, o_cmp, small, qfeat, kfeat, onehot, vone)


def _merge_body(ys_ref, yb_ref, yn_ref, mg_ref, x_ref, ws_ref, wb_ref, wn_ref, wo_ref, g_ref, o_ref):
    d = D_MODEL
    mg = mg_ref[...].astype(F32)
    mixed = (_sigmoid(mg[:, 0:d]) * _dot(ys_ref[...], ws_ref[...])
             + _sigmoid(mg[:, d:2 * d]) * _dot(yb_ref[...], wb_ref[...])
             + _sigmoid(mg[:, 2 * d:3 * d]) * _dot(yn_ref[...], wn_ref[...]))
    out = _dot(mixed.astype(BF16), wo_ref[...])
    r = lax.rsqrt(jnp.mean(out * out, axis=-1, keepdims=True) + NORM_EPS)
    o_ref[...] = x_ref[...] + out * r * g_ref[...]


def _merge(x2d, proj2d, y_ssm, y_sb, y_nsa, w_ssm, w_sb, w_nsa_pad, w_out, gain, tm=512):
    t, d = x2d.shape
    row = lambda w: pl.BlockSpec((tm, w), lambda i: (i, 0))
    full = lambda a: pl.BlockSpec(a.shape, lambda i: (0, 0))
    gain = gain.reshape(1, d).astype(F32)
    return pl.pallas_call(
        _merge_body,
        grid=(t // tm,),
        in_specs=[row(y_ssm.shape[1]), row(y_sb.shape[1]), row(y_nsa.shape[1]),
                  pl.BlockSpec((tm, 3 * d), lambda i: (i, C_MG // (3 * d))),
                  row(d), full(w_ssm), full(w_sb), full(w_nsa_pad), full(w_out), full(gain)],
        out_specs=row(d),
        out_shape=jax.ShapeDtypeStruct((t, d), F32),
        compiler_params=_params("parallel"),
        name="merge",
    )(y_ssm, y_sb, y_nsa, proj2d, x2d, w_ssm, w_sb, w_nsa_pad, w_out, gain)


def _ffn_body(x_ref, xh_ref, gin_ref, wg_ref, wv_ref, cwg_ref, cwv_ref, cbg_ref, cbv_ref, wd_ref, gout_ref,
              o_ref, h_scr, acc_scr, *, tm, tiles_per_seq):
    i = pl.program_id(0)
    j = pl.program_id(1)
    halo = SUBLANES

    def norm(x):
        r = lax.rsqrt(jnp.mean(x * x, axis=-1, keepdims=True) + NORM_EPS)
        return (x * r * gin_ref[...]).astype(BF16)

    @pl.when(j == 0)
    def _():
        xh = jnp.where(i % tiles_per_seq == 0, 0.0, xh_ref[...])
        h_scr[0:halo, :] = norm(xh)
        h_scr[halo:, :] = norm(x_ref[...])

    h = h_scr[...]

    def conv(u, cw_ref, cb_ref):
        out = cb_ref[...] + cw_ref[FFN_CONV - 1:FFN_CONV, :] * u[halo:]
        for k in range(FFN_CONV - 1):
            sh = FFN_CONV - 1 - k
            out = out + cw_ref[k:k + 1, :] * pltpu.roll(u, sh, axis=0)[halo:]
        return out

    cg = conv(_dot(h, wg_ref[...]), cwg_ref, cbg_ref)
    cv = conv(_dot(h, wv_ref[...]), cwv_ref, cbv_ref)
    gelu = 0.5 * cg * (1.0 + jnp.tanh(0.7978845608028654 * (cg + 0.044715 * cg * cg * cg)))
    part = _dot((gelu * cv).astype(BF16), wd_ref[...])

    @pl.when(j == 0)
    def _():
        acc_scr[...] = part

    @pl.when(j > 0)
    def _():
        acc_scr[...] += part

    @pl.when(j == pl.num_programs(1) - 1)
    def _():
        f = acc_scr[...]
        r = lax.rsqrt(jnp.mean(f * f, axis=-1, keepdims=True) + NORM_EPS)
        o_ref[...] = x_ref[...] + f * r * gout_ref[...]


def _ffn(x2d, s, gain_in, w_up, conv_w, conv_b, w_down, gain_out, tm=1024, tn=256):
    t, d = x2d.shape
    nj = D_FF // tn
    halo = SUBLANES
    cw = conv_w.astype(F32)
    cb = conv_b.reshape(1, -1).astype(F32)
    w_up = w_up.astype(BF16)
    row1 = lambda a: a.reshape(1, d).astype(F32)
    return pl.pallas_call(
        functools.partial(_ffn_body, tm=tm, tiles_per_seq=s // tm),
        grid=(t // tm, nj),
        in_specs=[pl.BlockSpec((tm, d), lambda i, j: (i, 0)),
                  pl.BlockSpec((halo, d), lambda i, j: (jnp.maximum(i * (tm // halo) - 1, 0), 0)),
                  pl.BlockSpec((1, d), lambda i, j: (0, 0)),
                  pl.BlockSpec((d, tn), lambda i, j: (0, j)),
                  pl.BlockSpec((d, tn), lambda i, j: (0, nj + j)),
                  pl.BlockSpec((FFN_CONV, tn), lambda i, j: (0, j)),
                  pl.BlockSpec((FFN_CONV, tn), lambda i, j: (0, nj + j)),
                  pl.BlockSpec((1, tn), lambda i, j: (0, j)),
                  pl.BlockSpec((1, tn), lambda i, j: (0, nj + j)),
                  pl.BlockSpec((tn, d), lambda i, j: (j, 0)),
                  pl.BlockSpec((1, d), lambda i, j: (0, 0))],
        out_specs=pl.BlockSpec((tm, d), lambda i, j: (i, 0)),
        out_shape=jax.ShapeDtypeStruct((t, d), F32),
        scratch_shapes=[pltpu.VMEM((halo + tm, d), BF16), pltpu.VMEM((tm, d), F32)],
        compiler_params=_params("parallel", "arbitrary"),
        name="ffn",
    )(x2d, x2d, row1(gain_in), w_up, w_up, cw, cw, cb, cb, w_down.astype(BF16), row1(gain_out))


def _pack_w_in(w):
    d = w.shape[0]
    o = 0
    z = w[:, o:o + 1024]; o += 1024
    xs = w[:, o:o + 1024]; o += 1024
    bc = w[:, o:o + 1024]; o += 1024
    dt = w[:, o:o + SSM_HEADS]; o += SSM_HEADS
    sb = w[:, o:o + 1536]; o += 1536
    n_sbq = SB_HEADS * SB_HEAD_DIM
    sb = jnp.concatenate([sb[:, :n_sbq] * SB_HEAD_DIM ** -0.5, sb[:, n_sbq:]], axis=1)
    nq = w[:, o:o + 512]; o += 512
    nkv = w[:, o:o + 768]; o += 768
    ng = w[:, o:o + 24]; o += 24
    mg = w[:, o:o + 3072]
    hd = NSA_HEAD_DIM
    pad64 = lambda a: jnp.pad(a.reshape(d, -1, hd), ((0, 0), (0, 0), (0, LANES - hd))).reshape(d, -1)
    nq_pad = pad64(nq * (NSA_HEAD_DIM ** -0.5 * LOG2E))
    kv_pad = pad64(nkv[:, 2 * NSA_KV_HEADS * hd:])
    cmp_raw = nkv[:, :2 * NSA_KV_HEADS * hd]
    main = jnp.concatenate([z, xs, bc, mg, nq_pad, kv_pad, sb, cmp_raw], axis=1)
    ng3 = ng.reshape(d, 3, NSA_KV_HEADS, NSA_HG)
    gate_blocks = [jnp.pad(ng3[:, :, g, :].reshape(d, 3 * NSA_HG), ((0, 0), (0, LANES - 3 * NSA_HG)))
                   for g in range(NSA_KV_HEADS)]
    small = jnp.concatenate([jnp.pad(dt, ((0, 0), (0, LANES - SSM_HEADS)))] + gate_blocks, axis=1)
    return main.astype(BF16), small.astype(BF16)


def kernel(x, pre_mix_norm, w_in, ssm_conv_w, ssm_conv_b, ssm_dt_bias, ssm_a_log, ssm_d, ssm_norm, cmp_pos_k, cmp_w1_k, cmp_w2_k, cmp_pos_v, cmp_w1_v, cmp_w2_v, w_br_ssm, w_br_sb, w_br_nsa, w_out, post_mix_norm, pre_ffn_norm, ffn_w_up, ffn_conv_w, ffn_conv_b, ffn_w_down, post_ffn_norm):
    b, s, d = x.shape
    t = b * s
    x2d = x.reshape(t, d)
    hd = NSA_HEAD_DIM
    for l in range(w_in.shape[0]):
        w_main, w_small = _pack_w_in(w_in[l])
        proj = _norm_matmul(x2d, pre_mix_norm[l], w_main, BF16, tm=1024, tn=3328).reshape(b, s, N_MAIN)
        small = _norm_matmul(x2d, pre_mix_norm[l], w_small, F32, tm=1024, tn=N_SMALL).reshape(b, s, N_SMALL)
        y_ssm = _ssd(proj, small, ssm_conv_w[l], ssm_conv_b[l], ssm_dt_bias[l], ssm_a_log[l], ssm_d[l], ssm_norm[l])
        y_sb = _sb_attention(proj)
        kvc = _nsa_compress(proj, cmp_pos_k[l], cmp_w1_k[l], cmp_w2_k[l], cmp_pos_v[l], cmp_w1_v[l], cmp_w2_v[l])
        o_cmp, sel, flags = _nsa_cmp(proj, kvc, tq=NSA_TILE, tk=NSA_TILE)
        y_nsa = _nsa_main(proj, small, o_cmp, sel, flags, t=NSA_TILE)
        w_nsa_pad = jnp.pad(w_br_nsa[l].reshape(NSA_Q_HEADS, hd, d), ((0, 0), (0, LANES - hd), (0, 0)))
        x2d = _merge(x2d, proj.reshape(t, N_MAIN), y_ssm.reshape(t, -1), y_sb.reshape(t, -1), y_nsa.reshape(t, -1),
                     w_br_ssm[l].astype(BF16), w_br_sb[l].astype(BF16),
                     w_nsa_pad.reshape(NSA_Q_HEADS * LANES, d).astype(BF16), w_out[l].astype(BF16), post_mix_norm[l])
        x2d = _ffn(x2d, s, pre_ffn_norm[l], ffn_w_up[l], ffn_conv_w[l], ffn_conv_b[l], ffn_w_down[l],
                   post_ffn_norm[l])
    return x2d.reshape(b, s, d)
```

```python
import functools

import jax
import jax.numpy as jnp
from jax import lax
from jax.experimental import pallas as pl
from jax.experimental.pallas import tpu as pltpu

F32 = jnp.float32
BF16 = jnp.bfloat16

D_MODEL = 1024
SSM_D_INNER = 1024
SSM_HEAD_DIM = 64
SSM_HEADS = 16
SSM_GROUPS = 4
SSM_D_STATE = 128
SSM_CONV = 4
SSM_CHUNK = 128
SB_HEAD_DIM = 128
SB_HEADS = 4
NSA_HEAD_DIM = 64
NSA_Q_HEADS = 8
NSA_KV_HEADS = 2
NSA_HG = NSA_Q_HEADS // NSA_KV_HEADS
CMP_STRIDE = 16
CMP_LEN = 32
SEL_BLOCK = 64
N_SEL = 8
WINDOW = 512
D_FF = 2816
FFN_CONV = 3
NORM_EPS = 1e-6
NEG = -1e30
BIG = 1e30

LANES = 128
SUBLANES = 8
VMEM_LIMIT = 56 * 1024 * 1024

SB_CUTOFF = 104.0
LOG2E = 1.4426950408889634
NSA_TILE = 256

C_Z, C_XS, C_BC, C_MG, C_NQ, C_NKV, C_SB = 0, 1024, 2048, 3072, 6144, 7168, 8192
N_MAIN = 9728
S_CMP, S_DT, S_GATE = 0, 2, 3
N_SMALL = 640


def _dot(a, b):
    return jnp.dot(a, b, preferred_element_type=F32)


def _dot_nt(a, b):
    return lax.dot_general(a, b, (((1,), (1,)), ((), ())), preferred_element_type=F32)


def _split2(x):
    hi = x.astype(BF16)
    lo = (x - hi.astype(F32)).astype(BF16)
    return hi, lo


def _split3(x):
    hi = x.astype(BF16)
    r = x - hi.astype(F32)
    mid = r.astype(BF16)
    lo = (r - mid.astype(F32)).astype(BF16)
    return hi, mid, lo


def _sigmoid(x):
    return 1.0 / (1.0 + jnp.exp(-x))


def _softplus(x):
    return jnp.maximum(x, 0.0) + jnp.log1p(jnp.exp(-jnp.abs(x)))


def _params(*sem):
    return pltpu.CompilerParams(dimension_semantics=sem, vmem_limit_bytes=VMEM_LIMIT)


def _norm_matmul_body(x_ref, g_ref, w_ref, o_ref, h_ref):
    @pl.when(pl.program_id(1) == 0)
    def _():
        x = x_ref[...]
        r = lax.rsqrt(jnp.mean(x * x, axis=-1, keepdims=True) + NORM_EPS)
        h_ref[...] = (x * r * g_ref[...]).astype(BF16)

    o_ref[...] = _dot(h_ref[...], w_ref[...]).astype(o_ref.dtype)


def _norm_matmul(x2d, gain, w, out_dtype, tm, tn):
    t, d = x2d.shape
    n = w.shape[1]
    return pl.pallas_call(
        _norm_matmul_body,
        grid=(t // tm, n // tn),
        in_specs=[pl.BlockSpec((tm, d), lambda i, j: (i, 0)),
                  pl.BlockSpec((1, d), lambda i, j: (0, 0)),
                  pl.BlockSpec((d, tn), lambda i, j: (0, j))],
        out_specs=pl.BlockSpec((tm, tn), lambda i, j: (i, j)),
        out_shape=jax.ShapeDtypeStruct((t, n), out_dtype),
        scratch_shapes=[pltpu.VMEM((tm, d), BF16)],
        compiler_params=_params("parallel", "arbitrary"),
        name="norm_matmul",
    )(x2d, gain.reshape(1, d).astype(F32), w)


def _ssd_body(z_ref, xs_ref, bc_ref, sm_ref, cw_ref, cb_ref, dtb_ref, alog_ref, dexp_ref, nw_ref,
              o_ref, xbuf, state, ybuf):
    q = SSM_CHUNK
    c = pl.program_id(1)

    @pl.when(c == 0)
    def _():
        xbuf[0:SUBLANES, :] = jnp.zeros((SUBLANES, 2 * SSM_D_INNER), F32)
        state[...] = jnp.zeros_like(state)

    @pl.when(c > 0)
    def _():
        xbuf[0:SUBLANES, :] = xbuf[q:q + SUBLANES, :]

    xbuf[SUBLANES:SUBLANES + q, 0:SSM_D_INNER] = xs_ref[0].astype(F32)
    xbuf[SUBLANES:SUBLANES + q, SSM_D_INNER:] = bc_ref[0].astype(F32)
    xin = xbuf[...]
    conv = cb_ref[...] + cw_ref[SSM_CONV - 1:SSM_CONV, :] * xin[SUBLANES:]
    for k in range(SSM_CONV - 1):
        conv = conv + cw_ref[k:k + 1, :] * pltpu.roll(xin, SSM_CONV - 1 - k, axis=0)[SUBLANES:]
    xbc = conv * _sigmoid(conv)
    xs = xbc[:, :SSM_D_INNER]
    bm = xbc[:, SSM_D_INNER:SSM_D_INNER + SSM_GROUPS * SSM_D_STATE].astype(BF16)
    cm = xbc[:, SSM_D_INNER + SSM_GROUPS * SSM_D_STATE:].astype(BF16)

    dt = _softplus(sm_ref[0] + dtb_ref[...])
    a = -jnp.exp(alog_ref[...])
    dta = dt * a
    row = lax.broadcasted_iota(jnp.int32, (q, q), 0)
    col = lax.broadcasted_iota(jnp.int32, (q, q), 1)
    tri = row >= col
    tril = jnp.where(tri, 1.0, 0.0).astype(BF16)
    d1, d2, d3 = _split3(dta)
    acum = _dot(tril, d1) + _dot(tril, d2) + _dot(tril, d3)
    acum_t = acum.T
    dt_t = dt.T
    exp_acum = jnp.exp(acum)
    exp_last = exp_acum[q - 1:q, :]
    to_end_t = jnp.exp(acum_t[:, q - 1:q] - acum_t) * dt_t
    xs_t = xs.T
    xs_b = xs.astype(BF16)

    for g in range(SSM_GROUPS):
        bg = bm[:, g * SSM_D_STATE:(g + 1) * SSM_D_STATE]
        cg = cm[:, g * SSM_D_STATE:(g + 1) * SSM_D_STATE]
        cb = _dot_nt(cg, bg)
        for hh in range(SSM_HEADS // SSM_GROUPS):
            h = g * (SSM_HEADS // SSM_GROUPS) + hh
            lo, hi = h * SSM_HEAD_DIM, (h + 1) * SSM_HEAD_DIM
            seg = acum[:, h:h + 1] - acum_t[h:h + 1, :]
            decay = jnp.exp(jnp.where(tri, seg, NEG))
            w = (cb * decay * dt_t[h:h + 1, :]).astype(BF16)
            st = state[h]
            y = _dot(w, xs_b[:, lo:hi])
            y = y + _dot_nt(cg, st.astype(BF16)) * exp_acum[:, h:h + 1]
            ybuf[:, lo:hi] = y
            xw = (xs_t[lo:hi, :] * to_end_t[h:h + 1, :]).astype(BF16)
            state[h] = st * exp_last[:, h:h + 1] + _dot(xw, bg)

    y = ybuf[...] + dexp_ref[...] * xs
    z = z_ref[0].astype(F32)
    y = y * (z * _sigmoid(z))
    gw = SSM_D_INNER // SSM_GROUPS
    for g in range(SSM_GROUPS):
        yg = y[:, g * gw:(g + 1) * gw]
        r = lax.rsqrt(jnp.mean(yg * yg, axis=-1, keepdims=True) + NORM_EPS)
        o_ref[0, :, g * gw:(g + 1) * gw] = (yg * r * nw_ref[:, g * gw:(g + 1) * gw]).astype(o_ref.dtype)


def _ssd(proj, small, conv_w, conv_b, dt_bias, a_log, d_skip, norm_w):
    b, s, _ = proj.shape
    q = SSM_CHUNK
    wd = SSM_D_INNER
    pad = LANES - SSM_HEADS
    dtb = jnp.pad(dt_bias.astype(F32), (0, pad)).reshape(1, LANES)
    alog = jnp.pad(a_log.astype(F32), (0, pad)).reshape(1, LANES)
    dexp = jnp.repeat(d_skip.astype(F32), SSM_HEAD_DIM).reshape(1, wd)
    const = lambda shape: pl.BlockSpec(shape, lambda i, j: (0, 0))
    return pl.pallas_call(
        _ssd_body,
        grid=(b, s // q),
        in_specs=[pl.BlockSpec((1, q, wd), lambda i, j: (i, j, C_Z // wd)),
                  pl.BlockSpec((1, q, wd), lambda i, j: (i, j, C_XS // wd)),
                  pl.BlockSpec((1, q, wd), lambda i, j: (i, j, C_BC // wd)),
                  pl.BlockSpec((1, q, LANES), lambda i, j: (i, j, S_DT)),
                  const((SSM_CONV, 2 * wd)), const((1, 2 * wd)), const((1, LANES)), const((1, LANES)),
                  const((1, wd)), const((1, wd))],
        out_specs=pl.BlockSpec((1, q, wd), lambda i, j: (i, j, 0)),
        out_shape=jax.ShapeDtypeStruct((b, s, wd), BF16),
        scratch_shapes=[pltpu.VMEM((SUBLANES + q, 2 * wd), F32),
                        pltpu.VMEM((SSM_HEADS, SSM_HEAD_DIM, SSM_D_STATE), F32),
                        pltpu.VMEM((q, wd), F32)],
        compiler_params=_params("parallel", "arbitrary"),
        name="ssd",
    )(proj, proj, proj, small, conv_w.astype(F32), conv_b.reshape(1, -1).astype(F32), dtb, alog, dexp,
      norm_w.reshape(1, wd).astype(F32))


def _sb_body(q_ref, k_ref, v_ref, u_ref, o_ref, *, tq, tk):
    i = pl.program_id(1)
    nh, hd = SB_HEADS, SB_HEAD_DIM
    rows = nh * tq
    r1 = lax.broadcasted_iota(jnp.int32, (tq, tk), 0)
    c1 = lax.broadcasted_iota(jnp.int32, (tq, tk), 1)
    diag_bias = jnp.concatenate([jnp.where(c1 < r1, 0.0, NEG)] * nh, axis=0)
    u = u_ref[...]
    qs = [q_ref[0, :, h * hd:(h + 1) * hd] for h in range(nh)]

    def tile(j, carry, acc, bias):
        off = pl.multiple_of(j * tk, tk)
        z = jnp.concatenate(
            [_dot_nt(qs[h], k_ref[0, pl.ds(off, tk), h * hd:(h + 1) * hd]) for h in range(nh)], axis=0)
        if bias is not None:
            z = z + bias
        sp = jnp.maximum(z, 0.0) + jnp.log(1.0 + jnp.exp(-jnp.abs(z)))
        hi, lo = _split2(sp)
        later = carry - (_dot(hi, u) + _dot(lo, u))
        w = jnp.exp(z - sp + later).astype(BF16)
        acc = acc + jnp.concatenate(
            [_dot(w[h * tq:(h + 1) * tq], v_ref[0, pl.ds(off, tk), h * hd:(h + 1) * hd]) for h in range(nh)], axis=0)
        carry = later[:, 0:1] - sp[:, 0:1]
        return carry, acc

    def cond(st):
        j, carry, _ = st
        return jnp.logical_and(j >= 0, jnp.max(carry) > -SB_CUTOFF)

    def body(st):
        j, carry, acc = st
        carry, acc = tile(j, carry, acc, None)
        return j - 1, carry, acc

    carry, acc = tile(i, jnp.zeros((rows, 1), F32), jnp.zeros((rows, hd), F32), diag_bias)
    _, _, acc = lax.while_loop(cond, body, (i - 1, carry, acc))
    for h in range(nh):
        o_ref[0, :, h * hd:(h + 1) * hd] = acc[h * tq:(h + 1) * tq].astype(o_ref.dtype)


def _sb_attention(proj, tq=256, tk=256):
    assert tq == tk, "the diagonal key tile is assumed to coincide with the query tile"
    b, s, _ = proj.shape
    wd = SB_HEADS * SB_HEAD_DIM
    c0 = C_SB // wd
    r = lax.broadcasted_iota(jnp.int32, (tk, tk), 0)
    c = lax.broadcasted_iota(jnp.int32, (tk, tk), 1)
    u = (r > c).astype(BF16)
    return pl.pallas_call(
        functools.partial(_sb_body, tq=tq, tk=tk),
        grid=(b, s // tq),
        in_specs=[pl.BlockSpec((1, tq, wd), lambda i, j: (i, j, c0)),
                  pl.BlockSpec((1, s, wd), lambda i, j: (i, 0, c0 + 1)),
                  pl.BlockSpec((1, s, wd), lambda i, j: (i, 0, c0 + 2)),
                  pl.BlockSpec((tk, tk), lambda i, j: (0, 0))],
        out_specs=pl.BlockSpec((1, tq, wd), lambda i, j: (i, j, 0)),
        out_shape=jax.ShapeDtypeStruct((b, s, wd), BF16),
        compiler_params=_params("parallel", "arbitrary"),
        name="sb",
    )(proj, proj, proj, u)


def _nsa_compress_body(x_ref, w1_ref, pos_ref, w2_ref, o_ref, *, n16):
    first = jnp.zeros((n16, LANES), F32)
    second = jnp.zeros((n16, LANES), F32)
    for r in range(CMP_STRIDE):
        xr = x_ref[0, pl.ds(r, n16, stride=CMP_STRIDE), :]
        first = first + _dot((xr + pos_ref[0, r:r + 1, :]).astype(BF16), w1_ref[0, r])
        second = second + _dot((xr + pos_ref[0, CMP_STRIDE + r:CMP_STRIDE + r + 1, :]).astype(BF16),
                               w1_ref[0, CMP_STRIDE + r])
    pre = first + pltpu.roll(second, n16 - 1, axis=0)
    act = (pre * _sigmoid(pre)).astype(BF16)
    o_ref[0] = _dot(act, w2_ref[0]).astype(o_ref.dtype)


def _nsa_compress(small, pos_k, w1_k, w2_k, pos_v, w1_v, w2_v):
    b, s, _ = small.shape
    n16 = s // CMP_STRIDE
    hd = NSA_HEAD_DIM
    g = NSA_KV_HEADS
    assert g * hd == LANES, "one 128-lane block holds both kv heads of k_cmp (or v_cmp)"
    eye = jnp.eye(g, dtype=F32)
    w1 = jnp.stack([w1_k, w1_v]).reshape(2, CMP_LEN, hd, hd)
    w1 = jnp.einsum('ab,prde->pradbe', eye, w1).reshape(2, CMP_LEN, LANES, LANES).astype(BF16)
    w2 = jnp.pad(jnp.stack([w2_k, w2_v]), ((0, 0), (0, 0), (0, LANES - hd)))
    w2 = jnp.einsum('ab,pef->paebf', eye, w2).reshape(2, LANES, g * LANES).astype(BF16)
    pos = jnp.tile(jnp.stack([pos_k, pos_v]).astype(F32), (1, 1, g))
    return pl.pallas_call(
        functools.partial(_nsa_compress_body, n16=n16),
        grid=(b, 2),
        in_specs=[pl.BlockSpec((1, s, LANES), lambda i, p: (i, 0, S_CMP + p)),
                  pl.BlockSpec((1, CMP_LEN, LANES, LANES), lambda i, p: (p, 0, 0, 0)),
                  pl.BlockSpec((1, CMP_LEN, LANES), lambda i, p: (p, 0, 0)),
                  pl.BlockSpec((1, LANES, g * LANES), lambda i, p: (p, 0, 0))],
        out_specs=pl.BlockSpec((1, n16, g * LANES), lambda i, p: (i, 0, p)),
        out_shape=jax.ShapeDtypeStruct((b, n16, 2 * g * LANES), BF16),
        compiler_params=_params("parallel", "parallel"),
        name="nsa_compress",
    )(small, w1, pos, w2)


def _nsa_cmp_body(q_ref, kc_ref, vct_ref, qfeat_ref, cfeat_ref, poolt_ref, tpool_ref, oc_ref, sel_ref, flag_ref,
                  *, tq, n16, nb, n_sel):
    i = pl.program_id(1)
    tpos = i * tq + lax.broadcasted_iota(jnp.int32, (1, tq), 1)
    cmp_end = lax.broadcasted_iota(jnp.int32, (n16, 1), 0) * CMP_STRIDE + (CMP_LEN - 1)
    bias = jnp.where(cmp_end <= tpos, 0.0, NEG)
    has_past = jnp.where(tpos >= CMP_LEN - 1, 1.0, 0.0)
    bias4 = jnp.concatenate([bias] * NSA_HG, axis=1)
    has_past4 = jnp.concatenate([has_past] * NSA_HG, axis=1)
    blk = lax.broadcasted_iota(jnp.int32, (nb, 1), 0)
    blkf = blk.astype(F32)
    cur = tpos // SEL_BLOCK
    forced = (blk == 0) | (blk == cur) | (blk == cur - 1)
    valid = blk * SEL_BLOCK <= tpos
    poolt = poolt_ref[...]
    for g in range(NSA_KV_HEADS):
        kca = kc_ref[0, :, g * LANES:(g + 1) * LANES] + cfeat_ref[...]
        vct = vct_ref[0, g]
        heads = [g * NSA_HG + hh for hh in range(NSA_HG)]
        qa = jnp.concatenate(
            [q_ref[0, :, h * LANES:(h + 1) * LANES] + qfeat_ref[g, h - heads[0], 0:1, :] for h in heads], axis=0)
        sc = _dot_nt(kca, qa) + bias4
        e = jnp.exp2(sc - jnp.max(sc, axis=0, keepdims=True))
        p = e * (has_past4 / jnp.sum(e, axis=0, keepdims=True))
        oct = _dot(vct, p.astype(BF16)).astype(oc_ref.dtype)
        psum = jnp.zeros((n16, tq), F32)
        for hh, h in enumerate(heads):
            oc_ref[0, h] = oct[:, hh * tq:(hh + 1) * tq]
            psum = psum + p[:, hh * tq:(hh + 1) * tq]
        hi, lo = _split2(psum)
        imp = _dot(poolt, hi) + _dot(poolt, lo)
        val = jnp.where(valid, jnp.where(forced, BIG, imp), NEG)
        sel = jnp.zeros((nb, tq), F32)
        for _ in range(n_sel):
            mx = jnp.max(val, axis=0, keepdims=True)
            first = jnp.min(jnp.where(val == mx, blkf, float(nb)), axis=0, keepdims=True)
            hit = blkf == first
            sel = jnp.where(hit, 1.0, sel)
            val = jnp.where(hit, -jnp.inf, val)
        sel = jnp.where(valid, sel, 0.0).T
        sel_ref[0, g] = sel.astype(sel_ref.dtype)
        any_row = jnp.max(sel, axis=0, keepdims=True).astype(BF16)
        flag_ref[0, g, 0] = (_dot(any_row, tpool_ref[...]) > 0.5).astype(jnp.int32)


def _nsa_cmp(proj, kvc, tq, tk):
    b, s, _ = proj.shape
    n16 = s // CMP_STRIDE
    nb = s // SEL_BLOCK
    ratio = SEL_BLOCK // CMP_STRIDE
    n = lax.broadcasted_iota(jnp.int32, (nb, n16), 1)
    j = lax.broadcasted_iota(jnp.int32, (nb, n16), 0)
    poolt = ((n >= ratio * j - 1) & (n <= ratio * j + ratio - 1) & (n < n16 - 1)).astype(BF16)
    bi = lax.broadcasted_iota(jnp.int32, (nb, LANES), 0)
    ti = lax.broadcasted_iota(jnp.int32, (nb, LANES), 1)
    tpool = (bi // (tk // SEL_BLOCK) == ti).astype(BF16)
    qfeat, cfeat = _alibi_features(jnp.arange(n16, dtype=jnp.int32) * CMP_STRIDE + (CMP_LEN - 1))
    kw = NSA_KV_HEADS * LANES
    vct = jnp.swapaxes(kvc[:, :, kw:], -1, -2).reshape(b, NSA_KV_HEADS, LANES, n16)
    wq = NSA_Q_HEADS * LANES
    nq = s // tq
    const = lambda a: pl.BlockSpec(a.shape, lambda i, t: (0,) * a.ndim)
    return pl.pallas_call(
        functools.partial(_nsa_cmp_body, tq=tq, n16=n16, nb=nb, n_sel=min(N_SEL, nb)),
        grid=(b, nq),
        in_specs=[pl.BlockSpec((1, tq, wq), lambda i, t: (i, t, C_NQ // wq)),
                  pl.BlockSpec((1, n16, kw), lambda i, t: (i, 0, 0)),
                  pl.BlockSpec((1, NSA_KV_HEADS, LANES, n16), lambda i, t: (i, 0, 0, 0)),
                  const(qfeat), const(cfeat), const(poolt), const(tpool)],
        out_specs=[pl.BlockSpec((1, NSA_Q_HEADS, LANES, tq), lambda i, t: (i, 0, 0, t)),
                   pl.BlockSpec((1, NSA_KV_HEADS, tq, nb), lambda i, t: (i, 0, t, 0)),
                   pl.BlockSpec((1, NSA_KV_HEADS, 1, 1, LANES), lambda i, t: (i, 0, t, 0, 0))],
        out_shape=[jax.ShapeDtypeStruct((b, NSA_Q_HEADS, LANES, s), BF16),
                   jax.ShapeDtypeStruct((b, NSA_KV_HEADS, s, nb), BF16),
                   jax.ShapeDtypeStruct((b, NSA_KV_HEADS, nq, 1, LANES), jnp.int32)],
        compiler_params=_params("parallel", "parallel"),
        name="nsa_cmp",
    )(proj, kvc, vct, qfeat, cfeat, poolt, tpool)


def _nsa_main_body(flags_ref, q_ref, ks_ref, vs_ref, kw_ref, vw_ref, sel_ref, oc_ref, gate_ref,
                   qfeat_ref, kfeat_ref, onehot_ref, vone_ref, wbias_ref, o_ref, m_scr, acc_scr, *, t, nq):
    bi = pl.program_id(0)
    g = pl.program_id(1)
    i = pl.program_id(2)
    hg = NSA_HG
    hd = NSA_HEAD_DIM
    rows = hg * t
    fbase = ((bi * NSA_KV_HEADS + g) * nq + i) * nq
    qa = jnp.concatenate(
        [q_ref[0, :, h * LANES:(h + 1) * LANES] + qfeat_ref[0, h, 0:1, :] for h in range(hg)], axis=0)
    selb = ((sel_ref[0, 0].astype(F32) - 1.0) * BIG).astype(BF16)
    qs = jnp.concatenate([qa, jnp.concatenate([selb] * hg, axis=0)], axis=1)
    vone = vone_ref[...]

    def scores(qmat, ka, bias):
        sc = _dot_nt(qmat, ka)
        return (sc.reshape(hg, t, sc.shape[1]) + bias[None]).reshape(sc.shape)

    def update(sc, va, m_old, acc_old):
        m_new = jnp.maximum(m_old, jnp.max(sc, axis=-1, keepdims=True))
        p = jnp.exp2(sc - jnp.concatenate([m_new] * (sc.shape[1] // LANES), axis=1)).astype(BF16)
        return m_new, jnp.exp2(m_old - m_new) * acc_old + _dot(p, va)

    def ktile(ref, j, n):
        off = pl.multiple_of(j * t, t)
        return ref[0, pl.ds(off, n), :] + kfeat_ref[pl.ds(off, n), :]

    def vtile(ref, j, n):
        off = pl.multiple_of(j * t, t)
        return ref[0, pl.ds(off, n), :] + vone

    def ksel(j):
        off = pl.multiple_of(j * t, t)
        return jnp.concatenate([ktile(ks_ref, j, t), onehot_ref[pl.ds(off, t), :]], axis=1)

    m_scr[...] = jnp.full((rows, LANES), NEG, F32)
    acc_scr[...] = jnp.zeros((rows, LANES), F32)

    def sel_body(j, carry):
        @pl.when(flags_ref[fbase + j] > 0)
        def _():
            m_new, acc = update(_dot_nt(qs, ksel(j)), vtile(vs_ref, j, t), m_scr[...], acc_scr[...])
            m_scr[...] = m_new
            acc_scr[...] = acc
        return carry

    lax.fori_loop(0, i, sel_body, 0)
    causal = wbias_ref[0, :, 0:t]
    _, acc = update(scores(qs, ksel(i), causal), vtile(vs_ref, i, t), m_scr[...], acc_scr[...])
    o_sel = acc / acc[:, hd:hd + 1]

    w0 = jnp.maximum(i - 2, 0)
    sc = scores(qa, ktile(kw_ref, w0, 3 * t), wbias_ref[jnp.minimum(i, 2)])
    _, acc = update(sc, vtile(vw_ref, w0, 3 * t), jnp.full((rows, LANES), NEG, F32), jnp.zeros((rows, LANES), F32))
    o_win = acc / acc[:, hd:hd + 1]

    gates = _sigmoid(gate_ref[0])
    for h in range(hg):
        r0, r1 = h * t, (h + 1) * t
        oc = oc_ref[0, h].astype(F32).T
        out = (gates[:, h:h + 1] * oc + gates[:, hg + h:hg + h + 1] * o_sel[r0:r1]
               + gates[:, 2 * hg + h:2 * hg + h + 1] * o_win[r0:r1])
        o_ref[0, :, h * LANES:(h + 1) * LANES] = out.astype(o_ref.dtype)


def _alibi_features(pos):
    hd = NSA_HEAD_DIM
    kp = [(pos >> 8) << 8, ((pos >> 4) & 15) << 4, pos & 15]
    kfeat = jnp.zeros((pos.shape[0], LANES), F32)
    qfeat = jnp.zeros((NSA_Q_HEADS, LANES), F32)
    slope2 = jnp.asarray([LOG2E * 2.0 ** (-8.0 * (h + 1) / NSA_Q_HEADS) for h in range(NSA_Q_HEADS)], F32)
    qp = [p.astype(F32) for p in _split3(slope2)]
    for a in range(3):
        for bb in range(3):
            lane = hd + 3 * a + bb
            kfeat = kfeat.at[:, lane].set(kp[a].astype(F32))
            qfeat = qfeat.at[:, lane].set(qp[bb])
    qfeat = jnp.broadcast_to(qfeat.reshape(NSA_KV_HEADS, NSA_HG, 1, LANES), (NSA_KV_HEADS, NSA_HG, SUBLANES, LANES))
    return qfeat.astype(BF16), kfeat.astype(BF16)


def _nsa_features(s):
    nb = s // SEL_BLOCK
    pos = jnp.arange(s, dtype=jnp.int32)
    qfeat, kfeat = _alibi_features(pos)
    onehot = (pos[:, None] // SEL_BLOCK == jnp.arange(nb, dtype=jnp.int32)[None, :])
    vone = jnp.zeros((1, LANES), F32).at[0, NSA_HEAD_DIM].set(1.0)
    return qfeat, kfeat, onehot.astype(BF16), vone.astype(BF16)


def _nsa_main(proj, small, o_cmp, sel, flags, t):
    b, s, _ = proj.shape
    nb = s // SEL_BLOCK
    nq = s // t
    hg = NSA_HG
    wg = hg * LANES
    c_kv = C_NKV // LANES
    assert WINDOW == 2 * t and s >= 3 * t, "the window slab is the diagonal tile plus the two before it"
    qfeat, kfeat, onehot, vone = _nsa_features(s)
    r = lax.broadcasted_iota(jnp.int32, (t, t), 0)
    c = lax.broadcasted_iota(jnp.int32, (t, t), 1)
    zero = jnp.zeros((t, t), F32)
    neg = jnp.full((t, t), NEG, F32)
    cz = jnp.where(c <= r, 0.0, NEG)
    lw = jnp.where(c > r, 0.0, NEG)
    wbias = jnp.stack([jnp.concatenate(p, axis=1) for p in ((cz, neg, neg), (zero, cz, neg), (lw, zero, cz))])
    kv_spec = lambda part: pl.BlockSpec((1, s, LANES), lambda i, g, j, f: (i, 0, c_kv + 2 * part + g))
    grid_spec = pltpu.PrefetchScalarGridSpec(
        num_scalar_prefetch=1,
        grid=(b, NSA_KV_HEADS, nq),
        in_specs=[pl.BlockSpec((1, t, wg), lambda i, g, j, f: (i, j, C_NQ // wg + g)),
                  kv_spec(0), kv_spec(1), kv_spec(2), kv_spec(3),
                  pl.BlockSpec((1, 1, t, nb), lambda i, g, j, f: (i, g, j, 0)),
                  pl.BlockSpec((1, hg, LANES, t), lambda i, g, j, f: (i, g, 0, j)),
                  pl.BlockSpec((1, t, LANES), lambda i, g, j, f: (i, j, S_GATE + g)),
                  pl.BlockSpec((1, hg, SUBLANES, LANES), lambda i, g, j, f: (g, 0, 0, 0)),
                  pl.BlockSpec((s, LANES), lambda i, g, j, f: (0, 0)),
                  pl.BlockSpec((s, nb), lambda i, g, j, f: (0, 0)),
                  pl.BlockSpec((1, LANES), lambda i, g, j, f: (0, 0)),
                  pl.BlockSpec((3, t, 3 * t), lambda i, g, j, f: (0, 0, 0))],
        out_specs=pl.BlockSpec((1, t, wg), lambda i, g, j, f: (i, j, g)),
        scratch_shapes=[pltpu.VMEM((hg * t, LANES), F32), pltpu.VMEM((hg * t, LANES), F32)])
    flat = flags[:, :, :, 0, :nq].reshape(-1)
    return pl.pallas_call(
        functools.partial(_nsa_main_body, t=t, nq=nq),
        grid_spec=grid_spec,
        out_shape=jax.ShapeDtypeStruct((b, s, NSA_Q_HEADS * LANES), BF16),
        compiler_params=_params("parallel", "parallel", "arbitrary"),
        name="nsa_main",
    )(flat, proj, proj, proj, proj, proj, sel, o_cmp, small, qfeat, kfeat, onehot, vone, wbias)


def _merge_body(ys_ref, yb_ref, yn_ref, mg_ref, x_ref, ws_ref, wb_ref, wn_ref, wo_ref, g_ref, o_ref):
    d = D_MODEL
    mg = mg_ref[...].astype(F32)
    mixed = (_sigmoid(mg[:, 0:d]) * _dot(ys_ref[...], ws_ref[...])
             + _sigmoid(mg[:, d:2 * d]) * _dot(yb_ref[...], wb_ref[...])
             + _sigmoid(mg[:, 2 * d:3 * d]) * _dot(yn_ref[...], wn_ref[...]))
    out = _dot(mixed.astype(BF16), wo_ref[...])
    r = lax.rsqrt(jnp.mean(out * out, axis=-1, keepdims=True) + NORM_EPS)
    o_ref[...] = x_ref[...] + out * r * g_ref[...]


def _merge(x2d, proj2d, y_ssm, y_sb, y_nsa, w_ssm, w_sb, w_nsa_pad, w_out, gain, tm=512):
    t, d = x2d.shape
    row = lambda w: pl.BlockSpec((tm, w), lambda i: (i, 0))
    full = lambda a: pl.BlockSpec(a.shape, lambda i: (0, 0))
    gain = gain.reshape(1, d).astype(F32)
    return pl.pallas_call(
        _merge_body,
        grid=(t // tm,),
        in_specs=[row(y_ssm.shape[1]), row(y_sb.shape[1]), row(y_nsa.shape[1]),
                  pl.BlockSpec((tm, 3 * d), lambda i: (i, C_MG // (3 * d))),
                  row(d), full(w_ssm), full(w_sb), full(w_nsa_pad), full(w_out), full(gain)],
        out_specs=row(d),
        out_shape=jax.ShapeDtypeStruct((t, d), F32),
        compiler_params=_params("parallel"),
        name="merge",
    )(y_ssm, y_sb, y_nsa, proj2d, x2d, w_ssm, w_sb, w_nsa_pad, w_out, gain)


FFN_HALO = 16


def _ffn_body(x_ref, xh_ref, gin_ref, wup_ref, cw_ref, cb_ref, wd_ref, gout_ref, o_ref, *, tn, tiles_per_seq):
    i = pl.program_id(0)
    halo = FFN_HALO
    nj = D_FF // tn

    def norm(x):
        r = lax.rsqrt(jnp.mean(x * x, axis=-1, keepdims=True) + NORM_EPS)
        return (x * r * gin_ref[...]).astype(BF16)

    xh = jnp.where(i % tiles_per_seq == 0, 0.0, xh_ref[...])
    h = jnp.concatenate([norm(xh), norm(x_ref[...])], axis=0)

    def up(j):
        g0, v0 = j * tn, D_FF + j * tn
        return _dot(h, wup_ref[:, g0:g0 + tn]), _dot(h, wup_ref[:, v0:v0 + tn])

    def conv(u, c0):
        out = cb_ref[:, c0:c0 + tn] + cw_ref[FFN_CONV - 1:FFN_CONV, c0:c0 + tn] * u[halo:]
        for k in range(FFN_CONV - 1):
            out = out + cw_ref[k:k + 1, c0:c0 + tn] * pltpu.roll(u, FFN_CONV - 1 - k, axis=0)[halo:]
        return out

    acc = None
    nxt = up(0)
    for j in range(nj):
        ug, uv = nxt
        if j + 1 < nj:
            nxt = up(j + 1)
        cg = conv(ug, j * tn)
        cv = conv(uv, D_FF + j * tn)
        gelu = 0.5 * cg * (1.0 + jnp.tanh(0.7978845608028654 * (cg + 0.044715 * cg * cg * cg)))
        part = _dot((gelu * cv).astype(BF16), wd_ref[j * tn:(j + 1) * tn, :])
        acc = part if acc is None else acc + part
    r = lax.rsqrt(jnp.mean(acc * acc, axis=-1, keepdims=True) + NORM_EPS)
    o_ref[...] = x_ref[...] + acc * r * gout_ref[...]


def _ffn(x2d, s, gain_in, w_up, conv_w, conv_b, w_down, gain_out, tm=512, tn=256):
    t, d = x2d.shape
    halo = FFN_HALO
    cw = conv_w.astype(F32)
    cb = conv_b.reshape(1, -1).astype(F32)
    row1 = lambda a: a.reshape(1, d).astype(F32)
    full = lambda a: pl.BlockSpec(a.shape, lambda i: (0, 0))
    w_up = w_up.astype(BF16)
    w_down = w_down.astype(BF16)
    return pl.pallas_call(
        functools.partial(_ffn_body, tn=tn, tiles_per_seq=s // tm),
        grid=(t // tm,),
        in_specs=[pl.BlockSpec((tm, d), lambda i: (i, 0)),
                  pl.BlockSpec((halo, d), lambda i: (jnp.maximum(i * (tm // halo) - 1, 0), 0)),
                  pl.BlockSpec((1, d), lambda i: (0, 0)),
                  full(w_up), full(cw), full(cb), full(w_down),
                  pl.BlockSpec((1, d), lambda i: (0, 0))],
        out_specs=pl.BlockSpec((tm, d), lambda i: (i, 0)),
        out_shape=jax.ShapeDtypeStruct((t, d), F32),
        compiler_params=_params("parallel"),
        name="ffn",
    )(x2d, x2d, row1(gain_in), w_up, cw, cb, w_down, row1(gain_out))


def _pack_w_in(w):
    d = w.shape[0]
    o = 0
    z = w[:, o:o + 1024]; o += 1024
    xs = w[:, o:o + 1024]; o += 1024
    bc = w[:, o:o + 1024]; o += 1024
    dt = w[:, o:o + SSM_HEADS]; o += SSM_HEADS
    sb = w[:, o:o + 1536]; o += 1536
    n_sbq = SB_HEADS * SB_HEAD_DIM
    sb = jnp.concatenate([sb[:, :n_sbq] * SB_HEAD_DIM ** -0.5, sb[:, n_sbq:]], axis=1)
    nq = w[:, o:o + 512]; o += 512
    nkv = w[:, o:o + 768]; o += 768
    ng = w[:, o:o + 24]; o += 24
    mg = w[:, o:o + 3072]
    hd = NSA_HEAD_DIM
    pad64 = lambda a: jnp.pad(a.reshape(d, -1, hd), ((0, 0), (0, 0), (0, LANES - hd))).reshape(d, -1)
    nq_pad = pad64(nq * (NSA_HEAD_DIM ** -0.5 * LOG2E))
    kv_pad = pad64(nkv[:, 2 * NSA_KV_HEADS * hd:])
    cmp_raw = nkv[:, :2 * NSA_KV_HEADS * hd]
    main = jnp.concatenate([z, xs, bc, mg, nq_pad, kv_pad, sb], axis=1)
    ng3 = ng.reshape(d, 3, NSA_KV_HEADS, NSA_HG)
    gate_blocks = [jnp.pad(ng3[:, :, g, :].reshape(d, 3 * NSA_HG), ((0, 0), (0, LANES - 3 * NSA_HG)))
                   for g in range(NSA_KV_HEADS)]
    small = jnp.concatenate([cmp_raw, jnp.pad(dt, ((0, 0), (0, LANES - SSM_HEADS)))] + gate_blocks, axis=1)
    return main.astype(BF16), small.astype(BF16)


def kernel(x, pre_mix_norm, w_in, ssm_conv_w, ssm_conv_b, ssm_dt_bias, ssm_a_log, ssm_d, ssm_norm, cmp_pos_k, cmp_w1_k, cmp_w2_k, cmp_pos_v, cmp_w1_v, cmp_w2_v, w_br_ssm, w_br_sb, w_br_nsa, w_out, post_mix_norm, pre_ffn_norm, ffn_w_up, ffn_conv_w, ffn_conv_b, ffn_w_down, post_ffn_norm):
    b, s, d = x.shape
    t = b * s
    x2d = x.reshape(t, d)
    hd = NSA_HEAD_DIM
    for l in range(w_in.shape[0]):
        w_main, w_small = _pack_w_in(w_in[l])
        proj = _norm_matmul(x2d, pre_mix_norm[l], w_main, BF16, tm=1024, tn=2432).reshape(b, s, N_MAIN)
        small = _norm_matmul(x2d, pre_mix_norm[l], w_small, F32, tm=1024, tn=N_SMALL).reshape(b, s, N_SMALL)
        y_ssm = _ssd(proj, small, ssm_conv_w[l], ssm_conv_b[l], ssm_dt_bias[l], ssm_a_log[l], ssm_d[l], ssm_norm[l])
        y_sb = _sb_attention(proj)
        kvc = _nsa_compress(small, cmp_pos_k[l], cmp_w1_k[l], cmp_w2_k[l], cmp_pos_v[l], cmp_w1_v[l], cmp_w2_v[l])
        o_cmp, sel, flags = _nsa_cmp(proj, kvc, tq=NSA_TILE, tk=NSA_TILE)
        y_nsa = _nsa_main(proj, small, o_cmp, sel, flags, t=NSA_TILE)
        w_nsa_pad = jnp.pad(w_br_nsa[l].reshape(NSA_Q_HEADS, hd, d), ((0, 0), (0, LANES - hd), (0, 0)))
        x2d = _merge(x2d, proj.reshape(t, N_MAIN), y_ssm.reshape(t, -1), y_sb.reshape(t, -1), y_nsa.reshape(t, -1),
                     w_br_ssm[l].astype(BF16), w_br_sb[l].astype(BF16),
                     w_nsa_pad.reshape(NSA_Q_HEADS * LANES, d).astype(BF16), w_out[l].astype(BF16), post_mix_norm[l])
        x2d = _ffn(x2d, s, pre_ffn_norm[l], ffn_w_up[l], ffn_conv_w[l], ffn_conv_b[l], ffn_w_down[l],
                   post_ffn_norm[l])
    return x2d.reshape(b, s, d)
```

```python
import functools

import jax
import jax.numpy as jnp
from jax import lax
from jax.experimental import pallas as pl
from jax.experimental.pallas import tpu as pltpu

F32 = jnp.float32
BF16 = jnp.bfloat16

D_MODEL = 1024
SSM_D_INNER = 1024
SSM_HEAD_DIM = 64
SSM_HEADS = 16
SSM_GROUPS = 4
SSM_D_STATE = 128
SSM_CONV = 4
SSM_CHUNK = 128
SB_HEAD_DIM = 128
SB_HEADS = 4
NSA_HEAD_DIM = 64
NSA_Q_HEADS = 8
NSA_KV_HEADS = 2
NSA_HG = NSA_Q_HEADS // NSA_KV_HEADS
CMP_STRIDE = 16
CMP_LEN = 32
SEL_BLOCK = 64
N_SEL = 8
WINDOW = 512
D_FF = 2816
FFN_CONV = 3
NORM_EPS = 1e-6
NEG = -1e30
BIG = 1e30

LANES = 128
SUBLANES = 8
VMEM_LIMIT = 56 * 1024 * 1024

SB_CUTOFF = 104.0
LOG2E = 1.4426950408889634
NSA_TILE = 256

C_Z, C_XS, C_BC, C_MG, C_NQ, C_NKV, C_SB = 0, 1024, 2048, 3072, 6144, 7168, 8192
N_MAIN = 9728
S_CMP, S_DT, S_GATE = 0, 2, 3
N_SMALL = 640


def _dot(a, b):
    return jnp.dot(a, b, preferred_element_type=F32)


def _dot_nt(a, b):
    return lax.dot_general(a, b, (((1,), (1,)), ((), ())), preferred_element_type=F32)


def _split2(x):
    hi = x.astype(BF16)
    lo = (x - hi.astype(F32)).astype(BF16)
    return hi, lo


def _split3(x):
    hi = x.astype(BF16)
    r = x - hi.astype(F32)
    mid = r.astype(BF16)
    lo = (r - mid.astype(F32)).astype(BF16)
    return hi, mid, lo


def _sigmoid(x):
    return 1.0 / (1.0 + jnp.exp(-x))


def _softplus(x):
    return jnp.maximum(x, 0.0) + jnp.log1p(jnp.exp(-jnp.abs(x)))


def _params(*sem):
    return pltpu.CompilerParams(dimension_semantics=sem, vmem_limit_bytes=VMEM_LIMIT)


def _norm_matmul_body(x_ref, g_ref, w_ref, o_ref, h_ref):
    @pl.when(pl.program_id(1) == 0)
    def _():
        x = x_ref[...]
        r = lax.rsqrt(jnp.mean(x * x, axis=-1, keepdims=True) + NORM_EPS)
        h_ref[...] = (x * r * g_ref[...]).astype(BF16)

    o_ref[...] = _dot(h_ref[...], w_ref[...]).astype(o_ref.dtype)


def _norm_matmul(x2d, gain, w, out_dtype, tm, tn):
    t, d = x2d.shape
    n = w.shape[1]
    return pl.pallas_call(
        _norm_matmul_body,
        grid=(t // tm, n // tn),
        in_specs=[pl.BlockSpec((tm, d), lambda i, j: (i, 0)),
                  pl.BlockSpec((1, d), lambda i, j: (0, 0)),
                  pl.BlockSpec((d, tn), lambda i, j: (0, j))],
        out_specs=pl.BlockSpec((tm, tn), lambda i, j: (i, j)),
        out_shape=jax.ShapeDtypeStruct((t, n), out_dtype),
        scratch_shapes=[pltpu.VMEM((tm, d), BF16)],
        compiler_params=_params("parallel", "arbitrary"),
        name="norm_matmul",
    )(x2d, gain.reshape(1, d).astype(F32), w)


def _ssd_body(z_ref, xs_ref, bc_ref, sm_ref, cw_ref, cb_ref, dtb_ref, alog_ref, dexp_ref, nw_ref,
              o_ref, xbuf, state, ybuf):
    q = SSM_CHUNK
    c = pl.program_id(1)

    @pl.when(c == 0)
    def _():
        xbuf[0:SUBLANES, :] = jnp.zeros((SUBLANES, 2 * SSM_D_INNER), F32)
        state[...] = jnp.zeros_like(state)

    @pl.when(c > 0)
    def _():
        xbuf[0:SUBLANES, :] = xbuf[q:q + SUBLANES, :]

    xbuf[SUBLANES:SUBLANES + q, 0:SSM_D_INNER] = xs_ref[0].astype(F32)
    xbuf[SUBLANES:SUBLANES + q, SSM_D_INNER:] = bc_ref[0].astype(F32)
    xin = xbuf[...]
    conv = cb_ref[...] + cw_ref[SSM_CONV - 1:SSM_CONV, :] * xin[SUBLANES:]
    for k in range(SSM_CONV - 1):
        conv = conv + cw_ref[k:k + 1, :] * pltpu.roll(xin, SSM_CONV - 1 - k, axis=0)[SUBLANES:]
    xbc = conv * _sigmoid(conv)
    xs = xbc[:, :SSM_D_INNER]
    bm = xbc[:, SSM_D_INNER:SSM_D_INNER + SSM_GROUPS * SSM_D_STATE].astype(BF16)
    cm = xbc[:, SSM_D_INNER + SSM_GROUPS * SSM_D_STATE:].astype(BF16)

    dt = _softplus(sm_ref[0] + dtb_ref[...])
    a = -jnp.exp(alog_ref[...])
    dta = dt * a
    row = lax.broadcasted_iota(jnp.int32, (q, q), 0)
    col = lax.broadcasted_iota(jnp.int32, (q, q), 1)
    tri = row >= col
    tril = jnp.where(tri, 1.0, 0.0).astype(BF16)
    d1, d2, d3 = _split3(dta)
    acum = _dot(tril, d1) + _dot(tril, d2) + _dot(tril, d3)
    acum_t = acum.T
    dt_t = dt.T
    exp_acum = jnp.exp(acum)
    exp_last = exp_acum[q - 1:q, :]
    to_end_t = jnp.exp(acum_t[:, q - 1:q] - acum_t) * dt_t
    xs_t = xs.T
    xs_b = xs.astype(BF16)

    for g in range(SSM_GROUPS):
        bg = bm[:, g * SSM_D_STATE:(g + 1) * SSM_D_STATE]
        cg = cm[:, g * SSM_D_STATE:(g + 1) * SSM_D_STATE]
        cb = _dot_nt(cg, bg)
        for hh in range(SSM_HEADS // SSM_GROUPS):
            h = g * (SSM_HEADS // SSM_GROUPS) + hh
            lo, hi = h * SSM_HEAD_DIM, (h + 1) * SSM_HEAD_DIM
            seg = acum[:, h:h + 1] - acum_t[h:h + 1, :]
            decay = jnp.exp(jnp.where(tri, seg, NEG))
            w = (cb * decay * dt_t[h:h + 1, :]).astype(BF16)
            st = state[h]
            y = _dot(w, xs_b[:, lo:hi])
            y = y + _dot_nt(cg, st.astype(BF16)) * exp_acum[:, h:h + 1]
            ybuf[:, lo:hi] = y
            xw = (xs_t[lo:hi, :] * to_end_t[h:h + 1, :]).astype(BF16)
            state[h] = st * exp_last[:, h:h + 1] + _dot(xw, bg)

    y = ybuf[...] + dexp_ref[...] * xs
    z = z_ref[0].astype(F32)
    y = y * (z * _sigmoid(z))
    gw = SSM_D_INNER // SSM_GROUPS
    for g in range(SSM_GROUPS):
        yg = y[:, g * gw:(g + 1) * gw]
        r = lax.rsqrt(jnp.mean(yg * yg, axis=-1, keepdims=True) + NORM_EPS)
        o_ref[0, :, g * gw:(g + 1) * gw] = (yg * r * nw_ref[:, g * gw:(g + 1) * gw]).astype(o_ref.dtype)


def _ssd(proj, small, conv_w, conv_b, dt_bias, a_log, d_skip, norm_w):
    b, s, _ = proj.shape
    q = SSM_CHUNK
    wd = SSM_D_INNER
    pad = LANES - SSM_HEADS
    dtb = jnp.pad(dt_bias.astype(F32), (0, pad)).reshape(1, LANES)
    alog = jnp.pad(a_log.astype(F32), (0, pad)).reshape(1, LANES)
    dexp = jnp.repeat(d_skip.astype(F32), SSM_HEAD_DIM).reshape(1, wd)
    const = lambda shape: pl.BlockSpec(shape, lambda i, j: (0, 0))
    return pl.pallas_call(
        _ssd_body,
        grid=(b, s // q),
        in_specs=[pl.BlockSpec((1, q, wd), lambda i, j: (i, j, C_Z // wd)),
                  pl.BlockSpec((1, q, wd), lambda i, j: (i, j, C_XS // wd)),
                  pl.BlockSpec((1, q, wd), lambda i, j: (i, j, C_BC // wd)),
                  pl.BlockSpec((1, q, LANES), lambda i, j: (i, j, S_DT)),
                  const((SSM_CONV, 2 * wd)), const((1, 2 * wd)), const((1, LANES)), const((1, LANES)),
                  const((1, wd)), const((1, wd))],
        out_specs=pl.BlockSpec((1, q, wd), lambda i, j: (i, j, 0)),
        out_shape=jax.ShapeDtypeStruct((b, s, wd), BF16),
        scratch_shapes=[pltpu.VMEM((SUBLANES + q, 2 * wd), F32),
                        pltpu.VMEM((SSM_HEADS, SSM_HEAD_DIM, SSM_D_STATE), F32),
                        pltpu.VMEM((q, wd), F32)],
        compiler_params=_params("parallel", "arbitrary"),
        name="ssd",
    )(proj, proj, proj, small, conv_w.astype(F32), conv_b.reshape(1, -1).astype(F32), dtb, alog, dexp,
      norm_w.reshape(1, wd).astype(F32))


def _sb_body(q_ref, k_ref, v_ref, u_ref, o_ref, *, tq, tk):
    i = pl.program_id(1)
    nh, hd = SB_HEADS, SB_HEAD_DIM
    rows = nh * tq
    r1 = lax.broadcasted_iota(jnp.int32, (tq, tk), 0)
    c1 = lax.broadcasted_iota(jnp.int32, (tq, tk), 1)
    diag_bias = jnp.concatenate([jnp.where(c1 < r1, 0.0, NEG)] * nh, axis=0)
    u = u_ref[...]
    qs = [q_ref[0, :, h * hd:(h + 1) * hd] for h in range(nh)]

    def tile(j, carry, acc, bias):
        off = pl.multiple_of(j * tk, tk)
        z = jnp.concatenate(
            [_dot_nt(qs[h], k_ref[0, pl.ds(off, tk), h * hd:(h + 1) * hd]) for h in range(nh)], axis=0)
        if bias is not None:
            z = z + bias
        sp = jnp.maximum(z, 0.0) + jnp.log(1.0 + jnp.exp(-jnp.abs(z)))
        hi, lo = _split2(sp)
        later = carry - (_dot(hi, u) + _dot(lo, u))
        w = jnp.exp(z - sp + later).astype(BF16)
        acc = acc + jnp.concatenate(
            [_dot(w[h * tq:(h + 1) * tq], v_ref[0, pl.ds(off, tk), h * hd:(h + 1) * hd]) for h in range(nh)], axis=0)
        carry = later[:, 0:1] - sp[:, 0:1]
        return carry, acc

    def cond(st):
        j, carry, _ = st
        return jnp.logical_and(j >= 0, jnp.max(carry) > -SB_CUTOFF)

    def body(st):
        j, carry, acc = st
        carry, acc = tile(j, carry, acc, None)
        return j - 1, carry, acc

    carry, acc = tile(i, jnp.zeros((rows, 1), F32), jnp.zeros((rows, hd), F32), diag_bias)
    _, _, acc = lax.while_loop(cond, body, (i - 1, carry, acc))
    for h in range(nh):
        o_ref[0, :, h * hd:(h + 1) * hd] = acc[h * tq:(h + 1) * tq].astype(o_ref.dtype)


def _sb_attention(proj, tq=256, tk=256):
    assert tq == tk, "the diagonal key tile is assumed to coincide with the query tile"
    b, s, _ = proj.shape
    wd = SB_HEADS * SB_HEAD_DIM
    c0 = C_SB // wd
    r = lax.broadcasted_iota(jnp.int32, (tk, tk), 0)
    c = lax.broadcasted_iota(jnp.int32, (tk, tk), 1)
    u = (r > c).astype(BF16)
    return pl.pallas_call(
        functools.partial(_sb_body, tq=tq, tk=tk),
        grid=(b, s // tq),
        in_specs=[pl.BlockSpec((1, tq, wd), lambda i, j: (i, j, c0)),
                  pl.BlockSpec((1, s, wd), lambda i, j: (i, 0, c0 + 1)),
                  pl.BlockSpec((1, s, wd), lambda i, j: (i, 0, c0 + 2)),
                  pl.BlockSpec((tk, tk), lambda i, j: (0, 0))],
        out_specs=pl.BlockSpec((1, tq, wd), lambda i, j: (i, j, 0)),
        out_shape=jax.ShapeDtypeStruct((b, s, wd), BF16),
        compiler_params=_params("parallel", "arbitrary"),
        name="sb",
    )(proj, proj, proj, u)


def _nsa_compress_body(x_ref, w1_ref, pos_ref, w2_ref, o_ref, *, n16):
    first = jnp.zeros((n16, LANES), F32)
    second = jnp.zeros((n16, LANES), F32)
    for r in range(CMP_STRIDE):
        xr = x_ref[0, pl.ds(r, n16, stride=CMP_STRIDE), :]
        first = first + _dot((xr + pos_ref[0, r:r + 1, :]).astype(BF16), w1_ref[0, r])
        second = second + _dot((xr + pos_ref[0, CMP_STRIDE + r:CMP_STRIDE + r + 1, :]).astype(BF16),
                               w1_ref[0, CMP_STRIDE + r])
    pre = first + pltpu.roll(second, n16 - 1, axis=0)
    act = (pre * _sigmoid(pre)).astype(BF16)
    o_ref[0] = _dot(act, w2_ref[0]).astype(o_ref.dtype)


def _nsa_compress(small, pos_k, w1_k, w2_k, pos_v, w1_v, w2_v):
    b, s, _ = small.shape
    n16 = s // CMP_STRIDE
    hd = NSA_HEAD_DIM
    g = NSA_KV_HEADS
    assert g * hd == LANES, "one 128-lane block holds both kv heads of k_cmp (or v_cmp)"
    eye = jnp.eye(g, dtype=F32)
    w1 = jnp.stack([w1_k, w1_v]).reshape(2, CMP_LEN, hd, hd)
    w1 = jnp.einsum('ab,prde->pradbe', eye, w1).reshape(2, CMP_LEN, LANES, LANES).astype(BF16)
    w2 = jnp.pad(jnp.stack([w2_k, w2_v]), ((0, 0), (0, 0), (0, LANES - hd)))
    w2 = jnp.einsum('ab,pef->paebf', eye, w2).reshape(2, LANES, g * LANES).astype(BF16)
    pos = jnp.tile(jnp.stack([pos_k, pos_v]).astype(F32), (1, 1, g))
    return pl.pallas_call(
        functools.partial(_nsa_compress_body, n16=n16),
        grid=(b, 2),
        in_specs=[pl.BlockSpec((1, s, LANES), lambda i, p: (i, 0, S_CMP + p)),
                  pl.BlockSpec((1, CMP_LEN, LANES, LANES), lambda i, p: (p, 0, 0, 0)),
                  pl.BlockSpec((1, CMP_LEN, LANES), lambda i, p: (p, 0, 0)),
                  pl.BlockSpec((1, LANES, g * LANES), lambda i, p: (p, 0, 0))],
        out_specs=pl.BlockSpec((1, n16, g * LANES), lambda i, p: (i, 0, p)),
        out_shape=jax.ShapeDtypeStruct((b, n16, 2 * g * LANES), BF16),
        compiler_params=_params("parallel", "parallel"),
        name="nsa_compress",
    )(small, w1, pos, w2)


def _nsa_cmp_body(q_ref, kc_ref, vct_ref, qfeat_ref, cfeat_ref, poolt_ref, tpool_ref, oc_ref, sel_ref, flag_ref,
                  *, tq, n16, nb, n_sel):
    i = pl.program_id(1)
    tpos = i * tq + lax.broadcasted_iota(jnp.int32, (1, tq), 1)
    has_past = jnp.where(tpos >= CMP_LEN - 1, 1.0, 0.0)
    has_past4 = jnp.concatenate([has_past] * NSA_HG, axis=1)
    blk = lax.broadcasted_iota(jnp.int32, (nb, 1), 0)
    blkf = blk.astype(F32)
    cur = tpos // SEL_BLOCK
    forced = (blk == 0) | (blk == cur) | (blk == cur - 1)
    valid = blk * SEL_BLOCK <= tpos

    def run(w):
        cmp_end = lax.broadcasted_iota(jnp.int32, (w, 1), 0) * CMP_STRIDE + (CMP_LEN - 1)
        bias = jnp.where(cmp_end <= tpos, 0.0, NEG)
        bias4 = jnp.concatenate([bias] * NSA_HG, axis=1)
        poolt = poolt_ref[:, 0:w]
        vals = []
        for g in range(NSA_KV_HEADS):
            kca = kc_ref[0, 0:w, g * LANES:(g + 1) * LANES] + cfeat_ref[0:w, :]
            vct = vct_ref[0, g, :, 0:w]
            heads = [g * NSA_HG + hh for hh in range(NSA_HG)]
            qa = jnp.concatenate(
                [q_ref[0, :, h * LANES:(h + 1) * LANES] + qfeat_ref[g, h - heads[0], 0:1, :] for h in heads], axis=0)
            sc = _dot_nt(kca, qa) + bias4
            e = jnp.exp2(sc - jnp.max(sc, axis=0, keepdims=True))
            p = e * (has_past4 / jnp.sum(e, axis=0, keepdims=True))
            oct = _dot(vct, p.astype(BF16)).astype(oc_ref.dtype)
            psum = jnp.zeros((w, tq), F32)
            for hh, h in enumerate(heads):
                oc_ref[0, h] = oct[:, hh * tq:(hh + 1) * tq]
                psum = psum + p[:, hh * tq:(hh + 1) * tq]
            hi, lo = _split2(psum)
            imp = _dot(poolt, hi) + _dot(poolt, lo)
            vals.append(jnp.where(valid, jnp.where(forced, BIG, imp), NEG))
        val = jnp.concatenate(vals, axis=1)
        sel2 = jnp.zeros(val.shape, F32)
        for _ in range(n_sel):
            mx = jnp.max(val, axis=0, keepdims=True)
            first = jnp.min(jnp.where(val == mx, blkf, float(nb)), axis=0, keepdims=True)
            hit = blkf == first
            sel2 = jnp.where(hit, 1.0, sel2)
            val = jnp.where(hit, -jnp.inf, val)
        for g in range(NSA_KV_HEADS):
            sel = jnp.where(valid, sel2[:, g * tq:(g + 1) * tq], 0.0).T
            sel_ref[0, g] = sel.astype(sel_ref.dtype)
            any_row = jnp.max(sel, axis=0, keepdims=True).astype(BF16)
            flag_ref[0, g, 0] = (_dot(any_row, tpool_ref[...]) > 0.5).astype(jnp.int32)

    cw = min(LANES, n16)
    total = n16 // cw
    need = jnp.minimum((i * tq + tq - CMP_LEN) // (CMP_STRIDE * cw) + 1, total)
    for c in range(1, total + 1):
        pl.when(need == c)(functools.partial(run, c * cw))


def _nsa_cmp(proj, kvc, tq, tk):
    b, s, _ = proj.shape
    n16 = s // CMP_STRIDE
    nb = s // SEL_BLOCK
    ratio = SEL_BLOCK // CMP_STRIDE
    n = lax.broadcasted_iota(jnp.int32, (nb, n16), 1)
    j = lax.broadcasted_iota(jnp.int32, (nb, n16), 0)
    poolt = ((n >= ratio * j - 1) & (n <= ratio * j + ratio - 1) & (n < n16 - 1)).astype(BF16)
    bi = lax.broadcasted_iota(jnp.int32, (nb, LANES), 0)
    ti = lax.broadcasted_iota(jnp.int32, (nb, LANES), 1)
    tpool = (bi // (tk // SEL_BLOCK) == ti).astype(BF16)
    qfeat, cfeat = _alibi_features(jnp.arange(n16, dtype=jnp.int32) * CMP_STRIDE + (CMP_LEN - 1))
    kw = NSA_KV_HEADS * LANES
    vct = jnp.swapaxes(kvc[:, :, kw:], -1, -2).reshape(b, NSA_KV_HEADS, LANES, n16)
    wq = NSA_Q_HEADS * LANES
    nq = s // tq
    const = lambda a: pl.BlockSpec(a.shape, lambda i, t: (0,) * a.ndim)
    return pl.pallas_call(
        functools.partial(_nsa_cmp_body, tq=tq, n16=n16, nb=nb, n_sel=min(N_SEL, nb)),
        grid=(b, nq),
        in_specs=[pl.BlockSpec((1, tq, wq), lambda i, t: (i, t, C_NQ // wq)),
                  pl.BlockSpec((1, n16, kw), lambda i, t: (i, 0, 0)),
                  pl.BlockSpec((1, NSA_KV_HEADS, LANES, n16), lambda i, t: (i, 0, 0, 0)),
                  const(qfeat), const(cfeat), const(poolt), const(tpool)],
        out_specs=[pl.BlockSpec((1, NSA_Q_HEADS, LANES, tq), lambda i, t: (i, 0, 0, t)),
                   pl.BlockSpec((1, NSA_KV_HEADS, tq, nb), lambda i, t: (i, 0, t, 0)),
                   pl.BlockSpec((1, NSA_KV_HEADS, 1, 1, LANES), lambda i, t: (i, 0, t, 0, 0))],
        out_shape=[jax.ShapeDtypeStruct((b, NSA_Q_HEADS, LANES, s), BF16),
                   jax.ShapeDtypeStruct((b, NSA_KV_HEADS, s, nb), BF16),
                   jax.ShapeDtypeStruct((b, NSA_KV_HEADS, nq, 1, LANES), jnp.int32)],
        compiler_params=_params("parallel", "parallel"),
        name="nsa_cmp",
    )(proj, kvc, vct, qfeat, cfeat, poolt, tpool)


def _nsa_main_body(flags_ref, q_ref, ks_ref, vs_ref, kw_ref, vw_ref, sel_ref, oc_ref, gate_ref,
                   qfeat_ref, kfeat_ref, onehot_ref, vone_ref, wbias_ref, o_ref, m_scr, acc_scr, *, t, nq):
    bi = pl.program_id(0)
    g = pl.program_id(1)
    i = pl.program_id(2)
    hg = NSA_HG
    hd = NSA_HEAD_DIM
    rows = hg * t
    fbase = ((bi * NSA_KV_HEADS + g) * nq + i) * nq
    qa = jnp.concatenate(
        [q_ref[0, :, h * LANES:(h + 1) * LANES] + qfeat_ref[0, h, 0:1, :] for h in range(hg)], axis=0)
    selb = ((sel_ref[0, 0].astype(F32) - 1.0) * BIG).astype(BF16)
    qs = jnp.concatenate([qa, jnp.concatenate([selb] * hg, axis=0)], axis=1)
    vone = vone_ref[...]

    def scores(qmat, ka, bias):
        sc = _dot_nt(qmat, ka)
        return (sc.reshape(hg, t, sc.shape[1]) + bias[None]).reshape(sc.shape)

    def update(sc, va, m_old, acc_old):
        m_new = jnp.maximum(m_old, jnp.max(sc, axis=-1, keepdims=True))
        p = jnp.exp2(sc - jnp.concatenate([m_new] * (sc.shape[1] // LANES), axis=1)).astype(BF16)
        return m_new, jnp.exp2(m_old - m_new) * acc_old + _dot(p, va)

    def ktile(ref, j, n):
        off = pl.multiple_of(j * t, t)
        return ref[0, pl.ds(off, n), :] + kfeat_ref[pl.ds(off, n), :]

    def vtile(ref, j, n):
        off = pl.multiple_of(j * t, t)
        return ref[0, pl.ds(off, n), :] + vone

    def ksel(j):
        off = pl.multiple_of(j * t, t)
        return jnp.concatenate([ktile(ks_ref, j, t), onehot_ref[pl.ds(off, t), :]], axis=1)

    m_scr[...] = jnp.full((rows, LANES), NEG, F32)
    acc_scr[...] = jnp.zeros((rows, LANES), F32)

    def sel_body(j, carry):
        @pl.when(flags_ref[fbase + j] > 0)
        def _():
            m_new, acc = update(_dot_nt(qs, ksel(j)), vtile(vs_ref, j, t), m_scr[...], acc_scr[...])
            m_scr[...] = m_new
            acc_scr[...] = acc
        return carry

    lax.fori_loop(0, i, sel_body, 0)
    causal = wbias_ref[0, :, 0:t]
    _, acc = update(scores(qs, ksel(i), causal), vtile(vs_ref, i, t), m_scr[...], acc_scr[...])
    o_sel = acc / acc[:, hd:hd + 1]

    w0 = jnp.maximum(i - 2, 0)
    sc = scores(qa, ktile(kw_ref, w0, 3 * t), wbias_ref[jnp.minimum(i, 2)])
    _, acc = update(sc, vtile(vw_ref, w0, 3 * t), jnp.full((rows, LANES), NEG, F32), jnp.zeros((rows, LANES), F32))
    o_win = acc / acc[:, hd:hd + 1]

    gates = _sigmoid(gate_ref[0])
    for h in range(hg):
        r0, r1 = h * t, (h + 1) * t
        oc = oc_ref[0, h].astype(F32).T
        out = (gates[:, h:h + 1] * oc + gates[:, hg + h:hg + h + 1] * o_sel[r0:r1]
               + gates[:, 2 * hg + h:2 * hg + h + 1] * o_win[r0:r1])
        o_ref[0, :, h * LANES:(h + 1) * LANES] = out.astype(o_ref.dtype)


def _alibi_features(pos):
    hd = NSA_HEAD_DIM
    kp = [(pos >> 8) << 8, ((pos >> 4) & 15) << 4, pos & 15]
    kfeat = jnp.zeros((pos.shape[0], LANES), F32)
    qfeat = jnp.zeros((NSA_Q_HEADS, LANES), F32)
    slope2 = jnp.asarray([LOG2E * 2.0 ** (-8.0 * (h + 1) / NSA_Q_HEADS) for h in range(NSA_Q_HEADS)], F32)
    qp = [p.astype(F32) for p in _split3(slope2)]
    for a in range(3):
        for bb in range(3):
            lane = hd + 3 * a + bb
            kfeat = kfeat.at[:, lane].set(kp[a].astype(F32))
            qfeat = qfeat.at[:, lane].set(qp[bb])
    qfeat = jnp.broadcast_to(qfeat.reshape(NSA_KV_HEADS, NSA_HG, 1, LANES), (NSA_KV_HEADS, NSA_HG, SUBLANES, LANES))
    return qfeat.astype(BF16), kfeat.astype(BF16)


def _nsa_features(s):
    nb = s // SEL_BLOCK
    pos = jnp.arange(s, dtype=jnp.int32)
    qfeat, kfeat = _alibi_features(pos)
    onehot = (pos[:, None] // SEL_BLOCK == jnp.arange(nb, dtype=jnp.int32)[None, :])
    vone = jnp.zeros((1, LANES), F32).at[0, NSA_HEAD_DIM].set(1.0)
    return qfeat, kfeat, onehot.astype(BF16), vone.astype(BF16)


def _nsa_main(proj, small, o_cmp, sel, flags, t):
    b, s, _ = proj.shape
    nb = s // SEL_BLOCK
    nq = s // t
    hg = NSA_HG
    wg = hg * LANES
    c_kv = C_NKV // LANES
    assert WINDOW == 2 * t and s >= 3 * t, "the window slab is the diagonal tile plus the two before it"
    qfeat, kfeat, onehot, vone = _nsa_features(s)
    r = lax.broadcasted_iota(jnp.int32, (t, t), 0)
    c = lax.broadcasted_iota(jnp.int32, (t, t), 1)
    zero = jnp.zeros((t, t), F32)
    neg = jnp.full((t, t), NEG, F32)
    cz = jnp.where(c <= r, 0.0, NEG)
    lw = jnp.where(c > r, 0.0, NEG)
    wbias = jnp.stack([jnp.concatenate(p, axis=1) for p in ((cz, neg, neg), (zero, cz, neg), (lw, zero, cz))])
    kv_spec = lambda part: pl.BlockSpec((1, s, LANES), lambda i, g, j, f: (i, 0, c_kv + 2 * part + g))
    grid_spec = pltpu.PrefetchScalarGridSpec(
        num_scalar_prefetch=1,
        grid=(b, NSA_KV_HEADS, nq),
        in_specs=[pl.BlockSpec((1, t, wg), lambda i, g, j, f: (i, j, C_NQ // wg + g)),
                  kv_spec(0), kv_spec(1), kv_spec(2), kv_spec(3),
                  pl.BlockSpec((1, 1, t, nb), lambda i, g, j, f: (i, g, j, 0)),
                  pl.BlockSpec((1, hg, LANES, t), lambda i, g, j, f: (i, g, 0, j)),
                  pl.BlockSpec((1, t, LANES), lambda i, g, j, f: (i, j, S_GATE + g)),
                  pl.BlockSpec((1, hg, SUBLANES, LANES), lambda i, g, j, f: (g, 0, 0, 0)),
                  pl.BlockSpec((s, LANES), lambda i, g, j, f: (0, 0)),
                  pl.BlockSpec((s, nb), lambda i, g, j, f: (0, 0)),
                  pl.BlockSpec((1, LANES), lambda i, g, j, f: (0, 0)),
                  pl.BlockSpec((3, t, 3 * t), lambda i, g, j, f: (0, 0, 0))],
        out_specs=pl.BlockSpec((1, t, wg), lambda i, g, j, f: (i, j, g)),
        scratch_shapes=[pltpu.VMEM((hg * t, LANES), F32), pltpu.VMEM((hg * t, LANES), F32)])
    flat = flags[:, :, :, 0, :nq].reshape(-1)
    return pl.pallas_call(
        functools.partial(_nsa_main_body, t=t, nq=nq),
        grid_spec=grid_spec,
        out_shape=jax.ShapeDtypeStruct((b, s, NSA_Q_HEADS * LANES), BF16),
        compiler_params=_params("parallel", "parallel", "arbitrary"),
        name="nsa_main",
    )(flat, proj, proj, proj, proj, proj, sel, o_cmp, small, qfeat, kfeat, onehot, vone, wbias)


def _merge_body(ys_ref, yb_ref, yn_ref, mg_ref, x_ref, ws_ref, wb_ref, wn_ref, wo_ref, g_ref, o_ref):
    d = D_MODEL
    mg = mg_ref[...].astype(F32)
    mixed = (_sigmoid(mg[:, 0:d]) * _dot(ys_ref[...], ws_ref[...])
             + _sigmoid(mg[:, d:2 * d]) * _dot(yb_ref[...], wb_ref[...])
             + _sigmoid(mg[:, 2 * d:3 * d]) * _dot(yn_ref[...], wn_ref[...]))
    out = _dot(mixed.astype(BF16), wo_ref[...])
    r = lax.rsqrt(jnp.mean(out * out, axis=-1, keepdims=True) + NORM_EPS)
    o_ref[...] = x_ref[...] + out * r * g_ref[...]


def _merge(x2d, proj2d, y_ssm, y_sb, y_nsa, w_ssm, w_sb, w_nsa_pad, w_out, gain, tm=512):
    t, d = x2d.shape
    row = lambda w: pl.BlockSpec((tm, w), lambda i: (i, 0))
    full = lambda a: pl.BlockSpec(a.shape, lambda i: (0, 0))
    gain = gain.reshape(1, d).astype(F32)
    return pl.pallas_call(
        _merge_body,
        grid=(t // tm,),
        in_specs=[row(y_ssm.shape[1]), row(y_sb.shape[1]), row(y_nsa.shape[1]),
                  pl.BlockSpec((tm, 3 * d), lambda i: (i, C_MG // (3 * d))),
                  row(d), full(w_ssm), full(w_sb), full(w_nsa_pad), full(w_out), full(gain)],
        out_specs=row(d),
        out_shape=jax.ShapeDtypeStruct((t, d), F32),
        compiler_params=_params("parallel"),
        name="merge",
    )(y_ssm, y_sb, y_nsa, proj2d, x2d, w_ssm, w_sb, w_nsa_pad, w_out, gain)


FFN_HALO = 16


def _ffn_body(x_ref, xh_ref, gin_ref, wup_ref, cw_ref, cb_ref, wd_ref, gout_ref, o_ref, *, tn, tiles_per_seq):
    i = pl.program_id(0)
    halo = FFN_HALO
    nj = D_FF // tn

    def norm(x):
        r = lax.rsqrt(jnp.mean(x * x, axis=-1, keepdims=True) + NORM_EPS)
        return (x * r * gin_ref[...]).astype(BF16)

    xh = jnp.where(i % tiles_per_seq == 0, 0.0, xh_ref[...])
    h = jnp.concatenate([norm(xh), norm(x_ref[...])], axis=0)

    def up(j):
        g0, v0 = j * tn, D_FF + j * tn
        return _dot(h, wup_ref[:, g0:g0 + tn]), _dot(h, wup_ref[:, v0:v0 + tn])

    def conv(u, c0):
        out = cb_ref[:, c0:c0 + tn] + cw_ref[FFN_CONV - 1:FFN_CONV, c0:c0 + tn] * u[halo:]
        for k in range(FFN_CONV - 1):
            out = out + cw_ref[k:k + 1, c0:c0 + tn] * pltpu.roll(u, FFN_CONV - 1 - k, axis=0)[halo:]
        return out

    acc = None
    nxt = up(0)
    for j in range(nj):
        ug, uv = nxt
        if j + 1 < nj:
            nxt = up(j + 1)
        cg = conv(ug, j * tn)
        cv = conv(uv, D_FF + j * tn)
        gelu = 0.5 * cg * (1.0 + jnp.tanh(0.7978845608028654 * (cg + 0.044715 * cg * cg * cg)))
        part = _dot((gelu * cv).astype(BF16), wd_ref[j * tn:(j + 1) * tn, :])
        acc = part if acc is None else acc + part
    r = lax.rsqrt(jnp.mean(acc * acc, axis=-1, keepdims=True) + NORM_EPS)
    o_ref[...] = x_ref[...] + acc * r * gout_ref[...]


def _ffn(x2d, s, gain_in, w_up, conv_w, conv_b, w_down, gain_out, tm=512, tn=256):
    t, d = x2d.shape
    halo = FFN_HALO
    cw = conv_w.astype(F32)
    cb = conv_b.reshape(1, -1).astype(F32)
    row1 = lambda a: a.reshape(1, d).astype(F32)
    full = lambda a: pl.BlockSpec(a.shape, lambda i: (0, 0))
    w_up = w_up.astype(BF16)
    w_down = w_down.astype(BF16)
    return pl.pallas_call(
        functools.partial(_ffn_body, tn=tn, tiles_per_seq=s // tm),
        grid=(t // tm,),
        in_specs=[pl.BlockSpec((tm, d), lambda i: (i, 0)),
                  pl.BlockSpec((halo, d), lambda i: (jnp.maximum(i * (tm // halo) - 1, 0), 0)),
                  pl.BlockSpec((1, d), lambda i: (0, 0)),
                  full(w_up), full(cw), full(cb), full(w_down),
                  pl.BlockSpec((1, d), lambda i: (0, 0))],
        out_specs=pl.BlockSpec((tm, d), lambda i: (i, 0)),
        out_shape=jax.ShapeDtypeStruct((t, d), F32),
        compiler_params=_params("parallel"),
        name="ffn",
    )(x2d, x2d, row1(gain_in), w_up, cw, cb, w_down, row1(gain_out))


def _pack_w_in(w):
    d = w.shape[0]
    o = 0
    z = w[:, o:o + 1024]; o += 1024
    xs = w[:, o:o + 1024]; o += 1024
    bc = w[:, o:o + 1024]; o += 1024
    dt = w[:, o:o + SSM_HEADS]; o += SSM_HEADS
    sb = w[:, o:o + 1536]; o += 1536
    n_sbq = SB_HEADS * SB_HEAD_DIM
    sb = jnp.concatenate([sb[:, :n_sbq] * SB_HEAD_DIM ** -0.5, sb[:, n_sbq:]], axis=1)
    nq = w[:, o:o + 512]; o += 512
    nkv = w[:, o:o + 768]; o += 768
    ng = w[:, o:o + 24]; o += 24
    mg = w[:, o:o + 3072]
    hd = NSA_HEAD_DIM
    pad64 = lambda a: jnp.pad(a.reshape(d, -1, hd), ((0, 0), (0, 0), (0, LANES - hd))).reshape(d, -1)
    nq_pad = pad64(nq * (NSA_HEAD_DIM ** -0.5 * LOG2E))
    kv_pad = pad64(nkv[:, 2 * NSA_KV_HEADS * hd:])
    cmp_raw = nkv[:, :2 * NSA_KV_HEADS * hd]
    main = jnp.concatenate([z, xs, bc, mg, nq_pad, kv_pad, sb], axis=1)
    ng3 = ng.reshape(d, 3, NSA_KV_HEADS, NSA_HG)
    gate_blocks = [jnp.pad(ng3[:, :, g, :].reshape(d, 3 * NSA_HG), ((0, 0), (0, LANES - 3 * NSA_HG)))
                   for g in range(NSA_KV_HEADS)]
    small = jnp.concatenate([cmp_raw, jnp.pad(dt, ((0, 0), (0, LANES - SSM_HEADS)))] + gate_blocks, axis=1)
    return main.astype(BF16), small.astype(BF16)


def kernel(x, pre_mix_norm, w_in, ssm_conv_w, ssm_conv_b, ssm_dt_bias, ssm_a_log, ssm_d, ssm_norm, cmp_pos_k, cmp_w1_k, cmp_w2_k, cmp_pos_v, cmp_w1_v, cmp_w2_v, w_br_ssm, w_br_sb, w_br_nsa, w_out, post_mix_norm, pre_ffn_norm, ffn_w_up, ffn_conv_w, ffn_conv_b, ffn_w_down, post_ffn_norm):
    b, s, d = x.shape
    t = b * s
    x2d = x.reshape(t, d)
    hd = NSA_HEAD_DIM
    for l in range(w_in.shape[0]):
        w_main, w_small = _pack_w_in(w_in[l])
        proj = _norm_matmul(x2d, pre_mix_norm[l], w_main, BF16, tm=1024, tn=2432).reshape(b, s, N_MAIN)
        small = _norm_matmul(x2d, pre_mix_norm[l], w_small, F32, tm=1024, tn=N_SMALL).reshape(b, s, N_SMALL)
        y_ssm = _ssd(proj, small, ssm_conv_w[l], ssm_conv_b[l], ssm_dt_bias[l], ssm_a_log[l], ssm_d[l], ssm_norm[l])
        y_sb = _sb_attention(proj)
        kvc = _nsa_compress(small, cmp_pos_k[l], cmp_w1_k[l], cmp_w2_k[l], cmp_pos_v[l], cmp_w1_v[l], cmp_w2_v[l])
        o_cmp, sel, flags = _nsa_cmp(proj, kvc, tq=NSA_TILE, tk=NSA_TILE)
        y_nsa = _nsa_main(proj, small, o_cmp, sel, flags, t=NSA_TILE)
        w_nsa_pad = jnp.pad(w_br_nsa[l].reshape(NSA_Q_HEADS, hd, d), ((0, 0), (0, LANES - hd), (0, 0)))
        x2d = _merge(x2d, proj.reshape(t, N_MAIN), y_ssm.reshape(t, -1), y_sb.reshape(t, -1), y_nsa.reshape(t, -1),
                     w_br_ssm[l].astype(BF16), w_br_sb[l].astype(BF16),
                     w_nsa_pad.reshape(NSA_Q_HEADS * LANES, d).astype(BF16), w_out[l].astype(BF16), post_mix_norm[l])
        x2d = _ffn(x2d, s, pre_ffn_norm[l], ffn_w_up[l], ffn_conv_w[l], ffn_conv_b[l], ffn_w_down[l],
                   post_ffn_norm[l])
    return x2d.reshape(b, s, d)
```

```python
import functools

import jax
import jax.numpy as jnp
from jax import lax
from jax.experimental import pallas as pl
from jax.experimental.pallas import tpu as pltpu

F32 = jnp.float32
BF16 = jnp.bfloat16

D_MODEL = 1024
SSM_D_INNER = 1024
SSM_HEAD_DIM = 64
SSM_HEADS = 16
SSM_GROUPS = 4
SSM_D_STATE = 128
SSM_CONV = 4
SSM_CHUNK = 128
SB_HEAD_DIM = 128
SB_HEADS = 4
NSA_HEAD_DIM = 64
NSA_Q_HEADS = 8
NSA_KV_HEADS = 2
NSA_HG = NSA_Q_HEADS // NSA_KV_HEADS
CMP_STRIDE = 16
CMP_LEN = 32
SEL_BLOCK = 64
N_SEL = 8
WINDOW = 512
D_FF = 2816
FFN_CONV = 3
NORM_EPS = 1e-6
NEG = -1e30
BIG = 1e30

LANES = 128
SUBLANES = 8
VMEM_LIMIT = 56 * 1024 * 1024

SB_CUTOFF = 104.0
LOG2E = 1.4426950408889634
NSA_TILE = 256

C_Z, C_XS, C_BC, C_MG, C_NQ, C_NKV, C_SB = 0, 1024, 2048, 3072, 6144, 7168, 8192
N_MAIN = 9728
S_CMP, S_DT, S_GATE = 0, 2, 3
N_SMALL = 640


def _dot(a, b):
    return jnp.dot(a, b, preferred_element_type=F32)


def _dot_nt(a, b):
    return lax.dot_general(a, b, (((1,), (1,)), ((), ())), preferred_element_type=F32)


def _split2(x):
    hi = x.astype(BF16)
    lo = (x - hi.astype(F32)).astype(BF16)
    return hi, lo


def _split3(x):
    hi = x.astype(BF16)
    r = x - hi.astype(F32)
    mid = r.astype(BF16)
    lo = (r - mid.astype(F32)).astype(BF16)
    return hi, mid, lo


def _sigmoid(x):
    return 1.0 / (1.0 + jnp.exp(-x))


def _softplus(x):
    return jnp.maximum(x, 0.0) + jnp.log1p(jnp.exp(-jnp.abs(x)))


def _params(*sem):
    return pltpu.CompilerParams(dimension_semantics=sem, vmem_limit_bytes=VMEM_LIMIT)


def _norm_matmul_body(x_ref, g_ref, w_ref, o_ref, h_ref):
    @pl.when(pl.program_id(1) == 0)
    def _():
        x = x_ref[...]
        r = lax.rsqrt(jnp.mean(x * x, axis=-1, keepdims=True) + NORM_EPS)
        h_ref[...] = (x * r * g_ref[...]).astype(BF16)

    o_ref[...] = _dot(h_ref[...], w_ref[...]).astype(o_ref.dtype)


def _norm_matmul(x2d, gain, w, out_dtype, tm, tn):
    t, d = x2d.shape
    n = w.shape[1]
    return pl.pallas_call(
        _norm_matmul_body,
        grid=(t // tm, n // tn),
        in_specs=[pl.BlockSpec((tm, d), lambda i, j: (i, 0)),
                  pl.BlockSpec((1, d), lambda i, j: (0, 0)),
                  pl.BlockSpec((d, tn), lambda i, j: (0, j))],
        out_specs=pl.BlockSpec((tm, tn), lambda i, j: (i, j)),
        out_shape=jax.ShapeDtypeStruct((t, n), out_dtype),
        scratch_shapes=[pltpu.VMEM((tm, d), BF16)],
        compiler_params=_params("parallel", "arbitrary"),
        name="norm_matmul",
    )(x2d, gain.reshape(1, d).astype(F32), w)


def _ssd_body(z_ref, xs_ref, bc_ref, sm_ref, cw_ref, cb_ref, dtb_ref, alog_ref, dexp_ref, nw_ref,
              o_ref, xbuf, state, ybuf):
    q = SSM_CHUNK
    c = pl.program_id(1)

    @pl.when(c == 0)
    def _():
        xbuf[0:SUBLANES, :] = jnp.zeros((SUBLANES, 2 * SSM_D_INNER), F32)
        state[...] = jnp.zeros_like(state)

    @pl.when(c > 0)
    def _():
        xbuf[0:SUBLANES, :] = xbuf[q:q + SUBLANES, :]

    xbuf[SUBLANES:SUBLANES + q, 0:SSM_D_INNER] = xs_ref[0].astype(F32)
    xbuf[SUBLANES:SUBLANES + q, SSM_D_INNER:] = bc_ref[0].astype(F32)
    xin = xbuf[...]
    conv = cb_ref[...] + cw_ref[SSM_CONV - 1:SSM_CONV, :] * xin[SUBLANES:]
    for k in range(SSM_CONV - 1):
        conv = conv + cw_ref[k:k + 1, :] * pltpu.roll(xin, SSM_CONV - 1 - k, axis=0)[SUBLANES:]
    xbc = conv * _sigmoid(conv)
    xs = xbc[:, :SSM_D_INNER]
    bm = xbc[:, SSM_D_INNER:SSM_D_INNER + SSM_GROUPS * SSM_D_STATE].astype(BF16)
    cm = xbc[:, SSM_D_INNER + SSM_GROUPS * SSM_D_STATE:].astype(BF16)

    dt = _softplus(sm_ref[0] + dtb_ref[...])
    a = -jnp.exp(alog_ref[...])
    dta = dt * a
    row = lax.broadcasted_iota(jnp.int32, (q, q), 0)
    col = lax.broadcasted_iota(jnp.int32, (q, q), 1)
    tri = row >= col
    tril = jnp.where(tri, 1.0, 0.0).astype(BF16)
    d1, d2, d3 = _split3(dta)
    acum = _dot(tril, d1) + _dot(tril, d2) + _dot(tril, d3)
    acum_t = acum.T
    dt_t = dt.T
    exp_acum = jnp.exp(acum)
    exp_last = exp_acum[q - 1:q, :]
    to_end_t = jnp.exp(acum_t[:, q - 1:q] - acum_t) * dt_t
    xs_t = xs.T
    xs_b = xs.astype(BF16)

    for g in range(SSM_GROUPS):
        bg = bm[:, g * SSM_D_STATE:(g + 1) * SSM_D_STATE]
        cg = cm[:, g * SSM_D_STATE:(g + 1) * SSM_D_STATE]
        cb = _dot_nt(cg, bg)
        for hh in range(SSM_HEADS // SSM_GROUPS):
            h = g * (SSM_HEADS // SSM_GROUPS) + hh
            lo, hi = h * SSM_HEAD_DIM, (h + 1) * SSM_HEAD_DIM
            seg = acum[:, h:h + 1] - acum_t[h:h + 1, :]
            decay = jnp.exp(jnp.where(tri, seg, NEG))
            w = (cb * decay * dt_t[h:h + 1, :]).astype(BF16)
            st = state[h]
            y = _dot(w, xs_b[:, lo:hi])
            y = y + _dot_nt(cg, st.astype(BF16)) * exp_acum[:, h:h + 1]
            ybuf[:, lo:hi] = y
            xw = (xs_t[lo:hi, :] * to_end_t[h:h + 1, :]).astype(BF16)
            state[h] = st * exp_last[:, h:h + 1] + _dot(xw, bg)

    y = ybuf[...] + dexp_ref[...] * xs
    z = z_ref[0].astype(F32)
    y = y * (z * _sigmoid(z))
    gw = SSM_D_INNER // SSM_GROUPS
    for g in range(SSM_GROUPS):
        yg = y[:, g * gw:(g + 1) * gw]
        r = lax.rsqrt(jnp.mean(yg * yg, axis=-1, keepdims=True) + NORM_EPS)
        o_ref[0, :, g * gw:(g + 1) * gw] = (yg * r * nw_ref[:, g * gw:(g + 1) * gw]).astype(o_ref.dtype)


def _ssd(proj, small, conv_w, conv_b, dt_bias, a_log, d_skip, norm_w):
    b, s, _ = proj.shape
    q = SSM_CHUNK
    wd = SSM_D_INNER
    pad = LANES - SSM_HEADS
    dtb = jnp.pad(dt_bias.astype(F32), (0, pad)).reshape(1, LANES)
    alog = jnp.pad(a_log.astype(F32), (0, pad)).reshape(1, LANES)
    dexp = jnp.repeat(d_skip.astype(F32), SSM_HEAD_DIM).reshape(1, wd)
    const = lambda shape: pl.BlockSpec(shape, lambda i, j: (0, 0))
    return pl.pallas_call(
        _ssd_body,
        grid=(b, s // q),
        in_specs=[pl.BlockSpec((1, q, wd), lambda i, j: (i, j, C_Z // wd)),
                  pl.BlockSpec((1, q, wd), lambda i, j: (i, j, C_XS // wd)),
                  pl.BlockSpec((1, q, wd), lambda i, j: (i, j, C_BC // wd)),
                  pl.BlockSpec((1, q, LANES), lambda i, j: (i, j, S_DT)),
                  const((SSM_CONV, 2 * wd)), const((1, 2 * wd)), const((1, LANES)), const((1, LANES)),
                  const((1, wd)), const((1, wd))],
        out_specs=pl.BlockSpec((1, q, wd), lambda i, j: (i, j, 0)),
        out_shape=jax.ShapeDtypeStruct((b, s, wd), BF16),
        scratch_shapes=[pltpu.VMEM((SUBLANES + q, 2 * wd), F32),
                        pltpu.VMEM((SSM_HEADS, SSM_HEAD_DIM, SSM_D_STATE), F32),
                        pltpu.VMEM((q, wd), F32)],
        compiler_params=_params("parallel", "arbitrary"),
        name="ssd",
    )(proj, proj, proj, small, conv_w.astype(F32), conv_b.reshape(1, -1).astype(F32), dtb, alog, dexp,
      norm_w.reshape(1, wd).astype(F32))


def _sb_body(q_ref, k_ref, v_ref, u_ref, o_ref, *, tq, tk):
    i = pl.program_id(1)
    nh, hd = SB_HEADS, SB_HEAD_DIM
    rows = nh * tq
    r1 = lax.broadcasted_iota(jnp.int32, (tq, tk), 0)
    c1 = lax.broadcasted_iota(jnp.int32, (tq, tk), 1)
    diag_bias = jnp.concatenate([jnp.where(c1 < r1, 0.0, NEG)] * nh, axis=0)
    u = u_ref[...]
    qs = [q_ref[0, :, h * hd:(h + 1) * hd] for h in range(nh)]

    def tile(j, carry, acc, bias):
        off = pl.multiple_of(j * tk, tk)
        z = jnp.concatenate(
            [_dot_nt(qs[h], k_ref[0, pl.ds(off, tk), h * hd:(h + 1) * hd]) for h in range(nh)], axis=0)
        if bias is not None:
            z = z + bias
        sp = jnp.maximum(z, 0.0) + jnp.log(1.0 + jnp.exp(-jnp.abs(z)))
        hi, lo = _split2(sp)
        later = carry - (_dot(hi, u) + _dot(lo, u))
        w = jnp.exp(z - sp + later).astype(BF16)
        acc = acc + jnp.concatenate(
            [_dot(w[h * tq:(h + 1) * tq], v_ref[0, pl.ds(off, tk), h * hd:(h + 1) * hd]) for h in range(nh)], axis=0)
        carry = later[:, 0:1] - sp[:, 0:1]
        return carry, acc

    def cond(st):
        j, carry, _ = st
        return jnp.logical_and(j >= 0, jnp.max(carry) > -SB_CUTOFF)

    def body(st):
        j, carry, acc = st
        carry, acc = tile(j, carry, acc, None)
        return j - 1, carry, acc

    carry, acc = tile(i, jnp.zeros((rows, 1), F32), jnp.zeros((rows, hd), F32), diag_bias)
    _, _, acc = lax.while_loop(cond, body, (i - 1, carry, acc))
    for h in range(nh):
        o_ref[0, :, h * hd:(h + 1) * hd] = acc[h * tq:(h + 1) * tq].astype(o_ref.dtype)


def _sb_attention(proj, tq=256, tk=256):
    assert tq == tk, "the diagonal key tile is assumed to coincide with the query tile"
    b, s, _ = proj.shape
    wd = SB_HEADS * SB_HEAD_DIM
    c0 = C_SB // wd
    r = lax.broadcasted_iota(jnp.int32, (tk, tk), 0)
    c = lax.broadcasted_iota(jnp.int32, (tk, tk), 1)
    u = (r > c).astype(BF16)
    return pl.pallas_call(
        functools.partial(_sb_body, tq=tq, tk=tk),
        grid=(b, s // tq),
        in_specs=[pl.BlockSpec((1, tq, wd), lambda i, j: (i, j, c0)),
                  pl.BlockSpec((1, s, wd), lambda i, j: (i, 0, c0 + 1)),
                  pl.BlockSpec((1, s, wd), lambda i, j: (i, 0, c0 + 2)),
                  pl.BlockSpec((tk, tk), lambda i, j: (0, 0))],
        out_specs=pl.BlockSpec((1, tq, wd), lambda i, j: (i, j, 0)),
        out_shape=jax.ShapeDtypeStruct((b, s, wd), BF16),
        compiler_params=_params("parallel", "arbitrary"),
        name="sb",
    )(proj, proj, proj, u)


def _nsa_compress_body(x_ref, w1_ref, pos_ref, w2_ref, o_ref, *, n16):
    first = jnp.zeros((n16, LANES), F32)
    second = jnp.zeros((n16, LANES), F32)
    for r in range(CMP_STRIDE):
        xr = x_ref[0, pl.ds(r, n16, stride=CMP_STRIDE), :]
        first = first + _dot((xr + pos_ref[0, r:r + 1, :]).astype(BF16), w1_ref[0, r])
        second = second + _dot((xr + pos_ref[0, CMP_STRIDE + r:CMP_STRIDE + r + 1, :]).astype(BF16),
                               w1_ref[0, CMP_STRIDE + r])
    pre = first + pltpu.roll(second, n16 - 1, axis=0)
    act = (pre * _sigmoid(pre)).astype(BF16)
    o_ref[0] = _dot(act, w2_ref[0]).astype(o_ref.dtype)


def _nsa_compress(small, pos_k, w1_k, w2_k, pos_v, w1_v, w2_v):
    b, s, _ = small.shape
    n16 = s // CMP_STRIDE
    hd = NSA_HEAD_DIM
    g = NSA_KV_HEADS
    assert g * hd == LANES, "one 128-lane block holds both kv heads of k_cmp (or v_cmp)"
    eye = jnp.eye(g, dtype=F32)
    w1 = jnp.stack([w1_k, w1_v]).reshape(2, CMP_LEN, hd, hd)
    w1 = jnp.einsum('ab,prde->pradbe', eye, w1).reshape(2, CMP_LEN, LANES, LANES).astype(BF16)
    w2 = jnp.pad(jnp.stack([w2_k, w2_v]), ((0, 0), (0, 0), (0, LANES - hd)))
    w2 = jnp.einsum('ab,pef->paebf', eye, w2).reshape(2, LANES, g * LANES).astype(BF16)
    pos = jnp.tile(jnp.stack([pos_k, pos_v]).astype(F32), (1, 1, g))
    return pl.pallas_call(
        functools.partial(_nsa_compress_body, n16=n16),
        grid=(b, 2),
        in_specs=[pl.BlockSpec((1, s, LANES), lambda i, p: (i, 0, S_CMP + p)),
                  pl.BlockSpec((1, CMP_LEN, LANES, LANES), lambda i, p: (p, 0, 0, 0)),
                  pl.BlockSpec((1, CMP_LEN, LANES), lambda i, p: (p, 0, 0)),
                  pl.BlockSpec((1, LANES, g * LANES), lambda i, p: (p, 0, 0))],
        out_specs=pl.BlockSpec((1, n16, g * LANES), lambda i, p: (i, 0, p)),
        out_shape=jax.ShapeDtypeStruct((b, n16, 2 * g * LANES), BF16),
        compiler_params=_params("parallel", "parallel"),
        name="nsa_compress",
    )(small, w1, pos, w2)


def _nsa_cmp_body(q_ref, kc_ref, vct_ref, qfeat_ref, cfeat_ref, poolt_ref, tpool_ref, oc_ref, sel_ref, flag_ref,
                  *, tq, n16, nb, n_sel):
    i = pl.program_id(1)
    tpos = i * tq + lax.broadcasted_iota(jnp.int32, (1, tq), 1)
    has_past = jnp.where(tpos >= CMP_LEN - 1, 1.0, 0.0)
    has_past4 = jnp.concatenate([has_past] * NSA_HG, axis=1)
    blk = lax.broadcasted_iota(jnp.int32, (nb, 1), 0)
    blkf = blk.astype(F32)
    cur = tpos // SEL_BLOCK
    forced = (blk == 0) | (blk == cur) | (blk == cur - 1)
    valid = blk * SEL_BLOCK <= tpos

    def run(w, distinct):
        cmp_end = lax.broadcasted_iota(jnp.int32, (w, 1), 0) * CMP_STRIDE + (CMP_LEN - 1)
        bias = jnp.where(cmp_end <= tpos, 0.0, NEG)
        bias4 = jnp.concatenate([bias] * NSA_HG, axis=1)
        poolt = poolt_ref[:, 0:w]
        vals = []
        for g in range(NSA_KV_HEADS):
            kca = kc_ref[0, 0:w, g * LANES:(g + 1) * LANES] + cfeat_ref[0:w, :]
            vct = vct_ref[0, g, :, 0:w]
            heads = [g * NSA_HG + hh for hh in range(NSA_HG)]
            qa = jnp.concatenate(
                [q_ref[0, :, h * LANES:(h + 1) * LANES] + qfeat_ref[g, h - heads[0], 0:1, :] for h in heads], axis=0)
            sc = _dot_nt(kca, qa) + bias4
            e = jnp.exp2(sc - jnp.max(sc, axis=0, keepdims=True))
            p = e * (has_past4 / jnp.sum(e, axis=0, keepdims=True))
            oct = _dot(vct, p.astype(BF16)).astype(oc_ref.dtype)
            psum = jnp.zeros((w, tq), F32)
            for hh, h in enumerate(heads):
                oc_ref[0, h] = oct[:, hh * tq:(hh + 1) * tq]
                psum = psum + p[:, hh * tq:(hh + 1) * tq]
            hi, lo = _split2(psum)
            imp = _dot(poolt, hi) + _dot(poolt, lo)
            vals.append(jnp.where(valid, jnp.where(forced, -jnp.inf if distinct else BIG, imp), NEG))
        val = jnp.concatenate(vals, axis=1)
        if distinct:
            sel2 = jnp.concatenate([jnp.where(forced, 1.0, 0.0)] * NSA_KV_HEADS, axis=1)
        else:
            sel2 = jnp.zeros(val.shape, F32)
        for _ in range(n_sel - 3 if distinct else n_sel):
            mx = jnp.max(val, axis=0, keepdims=True)
            first = jnp.min(jnp.where(val == mx, blkf, float(nb)), axis=0, keepdims=True)
            hit = blkf == first
            sel2 = jnp.where(hit, 1.0, sel2)
            val = jnp.where(hit, -jnp.inf, val)
        for g in range(NSA_KV_HEADS):
            sel = jnp.where(valid, sel2[:, g * tq:(g + 1) * tq], 0.0).T
            sel_ref[0, g] = sel.astype(sel_ref.dtype)
            any_row = jnp.max(sel, axis=0, keepdims=True).astype(BF16)
            flag_ref[0, g, 0] = (_dot(any_row, tpool_ref[...]) > 0.5).astype(jnp.int32)

    cw = min(LANES, n16)
    total = n16 // cw
    chunks = lambda tile: (tile * tq + tq - CMP_LEN) // (CMP_STRIDE * cw) + 1
    need = jnp.minimum(chunks(i), total)
    i0 = -(-2 * SEL_BLOCK // tq)
    for c in range(1, total + 1):
        pl.when((need == c) & (i >= i0))(functools.partial(run, c * cw, True))
        if c <= min(chunks(i0 - 1), total):
            pl.when((need == c) & (i < i0))(functools.partial(run, c * cw, False))


def _nsa_cmp(proj, kvc, tq, tk):
    b, s, _ = proj.shape
    n16 = s // CMP_STRIDE
    nb = s // SEL_BLOCK
    ratio = SEL_BLOCK // CMP_STRIDE
    n = lax.broadcasted_iota(jnp.int32, (nb, n16), 1)
    j = lax.broadcasted_iota(jnp.int32, (nb, n16), 0)
    poolt = ((n >= ratio * j - 1) & (n <= ratio * j + ratio - 1) & (n < n16 - 1)).astype(BF16)
    bi = lax.broadcasted_iota(jnp.int32, (nb, LANES), 0)
    ti = lax.broadcasted_iota(jnp.int32, (nb, LANES), 1)
    tpool = (bi // (tk // SEL_BLOCK) == ti).astype(BF16)
    qfeat, cfeat = _alibi_features(jnp.arange(n16, dtype=jnp.int32) * CMP_STRIDE + (CMP_LEN - 1))
    kw = NSA_KV_HEADS * LANES
    vct = jnp.swapaxes(kvc[:, :, kw:], -1, -2).reshape(b, NSA_KV_HEADS, LANES, n16)
    wq = NSA_Q_HEADS * LANES
    nq = s // tq
    assert min(N_SEL, nb) > 3, "the three forced blocks are pre-selected before the top-k rounds"
    const = lambda a: pl.BlockSpec(a.shape, lambda i, t: (0,) * a.ndim)
    return pl.pallas_call(
        functools.partial(_nsa_cmp_body, tq=tq, n16=n16, nb=nb, n_sel=min(N_SEL, nb)),
        grid=(b, nq),
        in_specs=[pl.BlockSpec((1, tq, wq), lambda i, t: (i, t, C_NQ // wq)),
                  pl.BlockSpec((1, n16, kw), lambda i, t: (i, 0, 0)),
                  pl.BlockSpec((1, NSA_KV_HEADS, LANES, n16), lambda i, t: (i, 0, 0, 0)),
                  const(qfeat), const(cfeat), const(poolt), const(tpool)],
        out_specs=[pl.BlockSpec((1, NSA_Q_HEADS, LANES, tq), lambda i, t: (i, 0, 0, t)),
                   pl.BlockSpec((1, NSA_KV_HEADS, tq, nb), lambda i, t: (i, 0, t, 0)),
                   pl.BlockSpec((1, NSA_KV_HEADS, 1, 1, LANES), lambda i, t: (i, 0, t, 0, 0))],
        out_shape=[jax.ShapeDtypeStruct((b, NSA_Q_HEADS, LANES, s), BF16),
                   jax.ShapeDtypeStruct((b, NSA_KV_HEADS, s, nb), BF16),
                   jax.ShapeDtypeStruct((b, NSA_KV_HEADS, nq, 1, LANES), jnp.int32)],
        compiler_params=_params("parallel", "parallel"),
        name="nsa_cmp",
    )(proj, kvc, vct, qfeat, cfeat, poolt, tpool)


def _nsa_main_body(flags_ref, q_ref, ks_ref, vs_ref, kw_ref, vw_ref, sel_ref, oc_ref, gate_ref,
                   qfeat_ref, kfeat_ref, onehot_ref, vone_ref, wbias_ref, o_ref, m_scr, acc_scr, *, t, nq):
    bi = pl.program_id(0)
    g = pl.program_id(1)
    i = pl.program_id(2)
    hg = NSA_HG
    hd = NSA_HEAD_DIM
    rows = hg * t
    fbase = ((bi * NSA_KV_HEADS + g) * nq + i) * nq
    qa = jnp.concatenate(
        [q_ref[0, :, h * LANES:(h + 1) * LANES] + qfeat_ref[0, h, 0:1, :] for h in range(hg)], axis=0)
    selb = ((sel_ref[0, 0].astype(F32) - 1.0) * BIG).astype(BF16)
    qs = jnp.concatenate([qa, jnp.concatenate([selb] * hg, axis=0)], axis=1)
    vone = vone_ref[...]

    def scores(qmat, ka, bias):
        sc = _dot_nt(qmat, ka)
        return (sc.reshape(hg, t, sc.shape[1]) + bias[None]).reshape(sc.shape)

    def update(sc, va, m_old, acc_old):
        m_new = jnp.maximum(m_old, jnp.max(sc, axis=-1, keepdims=True))
        p = jnp.exp2(sc - jnp.concatenate([m_new] * (sc.shape[1] // LANES), axis=1)).astype(BF16)
        return m_new, jnp.exp2(m_old - m_new) * acc_old + _dot(p, va)

    def ktile(ref, j, n):
        off = pl.multiple_of(j * t, t)
        return ref[0, pl.ds(off, n), :] + kfeat_ref[pl.ds(off, n), :]

    def vtile(ref, j, n):
        off = pl.multiple_of(j * t, t)
        return ref[0, pl.ds(off, n), :] + vone

    def ksel(j):
        off = pl.multiple_of(j * t, t)
        return jnp.concatenate([ktile(ks_ref, j, t), onehot_ref[pl.ds(off, t), :]], axis=1)

    m_scr[...] = jnp.full((rows, LANES), NEG, F32)
    acc_scr[...] = jnp.zeros((rows, LANES), F32)

    def sel_body(j, carry):
        @pl.when(flags_ref[fbase + j] > 0)
        def _():
            m_new, acc = update(_dot_nt(qs, ksel(j)), vtile(vs_ref, j, t), m_scr[...], acc_scr[...])
            m_scr[...] = m_new
            acc_scr[...] = acc
        return carry

    lax.fori_loop(0, i, sel_body, 0)
    causal = wbias_ref[0, :, 0:t]
    _, acc = update(scores(qs, ksel(i), causal), vtile(vs_ref, i, t), m_scr[...], acc_scr[...])
    o_sel = acc / acc[:, hd:hd + 1]

    w0 = jnp.maximum(i - 2, 0)
    sc = scores(qa, ktile(kw_ref, w0, 3 * t), wbias_ref[jnp.minimum(i, 2)])
    _, acc = update(sc, vtile(vw_ref, w0, 3 * t), jnp.full((rows, LANES), NEG, F32), jnp.zeros((rows, LANES), F32))
    o_win = acc / acc[:, hd:hd + 1]

    gates = _sigmoid(gate_ref[0])
    for h in range(hg):
        r0, r1 = h * t, (h + 1) * t
        oc = oc_ref[0, h].astype(F32).T
        out = (gates[:, h:h + 1] * oc + gates[:, hg + h:hg + h + 1] * o_sel[r0:r1]
               + gates[:, 2 * hg + h:2 * hg + h + 1] * o_win[r0:r1])
        o_ref[0, :, h * LANES:(h + 1) * LANES] = out.astype(o_ref.dtype)


def _alibi_features(pos):
    hd = NSA_HEAD_DIM
    kp = [(pos >> 8) << 8, ((pos >> 4) & 15) << 4, pos & 15]
    kfeat = jnp.zeros((pos.shape[0], LANES), F32)
    qfeat = jnp.zeros((NSA_Q_HEADS, LANES), F32)
    slope2 = jnp.asarray([LOG2E * 2.0 ** (-8.0 * (h + 1) / NSA_Q_HEADS) for h in range(NSA_Q_HEADS)], F32)
    qp = [p.astype(F32) for p in _split3(slope2)]
    for a in range(3):
        for bb in range(3):
            lane = hd + 3 * a + bb
            kfeat = kfeat.at[:, lane].set(kp[a].astype(F32))
            qfeat = qfeat.at[:, lane].set(qp[bb])
    qfeat = jnp.broadcast_to(qfeat.reshape(NSA_KV_HEADS, NSA_HG, 1, LANES), (NSA_KV_HEADS, NSA_HG, SUBLANES, LANES))
    return qfeat.astype(BF16), kfeat.astype(BF16)


def _nsa_features(s):
    nb = s // SEL_BLOCK
    pos = jnp.arange(s, dtype=jnp.int32)
    qfeat, kfeat = _alibi_features(pos)
    onehot = (pos[:, None] // SEL_BLOCK == jnp.arange(nb, dtype=jnp.int32)[None, :])
    vone = jnp.zeros((1, LANES), F32).at[0, NSA_HEAD_DIM].set(1.0)
    return qfeat, kfeat, onehot.astype(BF16), vone.astype(BF16)


def _nsa_main(proj, small, o_cmp, sel, flags, t):
    b, s, _ = proj.shape
    nb = s // SEL_BLOCK
    nq = s // t
    hg = NSA_HG
    wg = hg * LANES
    c_kv = C_NKV // LANES
    assert WINDOW == 2 * t and s >= 3 * t, "the window slab is the diagonal tile plus the two before it"
    qfeat, kfeat, onehot, vone = _nsa_features(s)
    r = lax.broadcasted_iota(jnp.int32, (t, t), 0)
    c = lax.broadcasted_iota(jnp.int32, (t, t), 1)
    zero = jnp.zeros((t, t), F32)
    neg = jnp.full((t, t), NEG, F32)
    cz = jnp.where(c <= r, 0.0, NEG)
    lw = jnp.where(c > r, 0.0, NEG)
    wbias = jnp.stack([jnp.concatenate(p, axis=1) for p in ((cz, neg, neg), (zero, cz, neg), (lw, zero, cz))])
    kv_spec = lambda part: pl.BlockSpec((1, s, LANES), lambda i, g, j, f: (i, 0, c_kv + 2 * part + g))
    grid_spec = pltpu.PrefetchScalarGridSpec(
        num_scalar_prefetch=1,
        grid=(b, NSA_KV_HEADS, nq),
        in_specs=[pl.BlockSpec((1, t, wg), lambda i, g, j, f: (i, j, C_NQ // wg + g)),
                  kv_spec(0), kv_spec(1), kv_spec(2), kv_spec(3),
                  pl.BlockSpec((1, 1, t, nb), lambda i, g, j, f: (i, g, j, 0)),
                  pl.BlockSpec((1, hg, LANES, t), lambda i, g, j, f: (i, g, 0, j)),
                  pl.BlockSpec((1, t, LANES), lambda i, g, j, f: (i, j, S_GATE + g)),
                  pl.BlockSpec((1, hg, SUBLANES, LANES), lambda i, g, j, f: (g, 0, 0, 0)),
                  pl.BlockSpec((s, LANES), lambda i, g, j, f: (0, 0)),
                  pl.BlockSpec((s, nb), lambda i, g, j, f: (0, 0)),
                  pl.BlockSpec((1, LANES), lambda i, g, j, f: (0, 0)),
                  pl.BlockSpec((3, t, 3 * t), lambda i, g, j, f: (0, 0, 0))],
        out_specs=pl.BlockSpec((1, t, wg), lambda i, g, j, f: (i, j, g)),
        scratch_shapes=[pltpu.VMEM((hg * t, LANES), F32), pltpu.VMEM((hg * t, LANES), F32)])
    flat = flags[:, :, :, 0, :nq].reshape(-1)
    return pl.pallas_call(
        functools.partial(_nsa_main_body, t=t, nq=nq),
        grid_spec=grid_spec,
        out_shape=jax.ShapeDtypeStruct((b, s, NSA_Q_HEADS * LANES), BF16),
        compiler_params=_params("parallel", "parallel", "arbitrary"),
        name="nsa_main",
    )(flat, proj, proj, proj, proj, proj, sel, o_cmp, small, qfeat, kfeat, onehot, vone, wbias)


def _merge_body(ys_ref, yb_ref, yn_ref, mg_ref, x_ref, ws_ref, wb_ref, wn_ref, wo_ref, g_ref, o_ref):
    d = D_MODEL
    mg = mg_ref[...].astype(F32)
    mixed = (_sigmoid(mg[:, 0:d]) * _dot(ys_ref[...], ws_ref[...])
             + _sigmoid(mg[:, d:2 * d]) * _dot(yb_ref[...], wb_ref[...])
             + _sigmoid(mg[:, 2 * d:3 * d]) * _dot(yn_ref[...], wn_ref[...]))
    out = _dot(mixed.astype(BF16), wo_ref[...])
    r = lax.rsqrt(jnp.mean(out * out, axis=-1, keepdims=True) + NORM_EPS)
    o_ref[...] = x_ref[...] + out * r * g_ref[...]


def _merge(x2d, proj2d, y_ssm, y_sb, y_nsa, w_ssm, w_sb, w_nsa_pad, w_out, gain, tm=512):
    t, d = x2d.shape
    row = lambda w: pl.BlockSpec((tm, w), lambda i: (i, 0))
    full = lambda a: pl.BlockSpec(a.shape, lambda i: (0, 0))
    gain = gain.reshape(1, d).astype(F32)
    return pl.pallas_call(
        _merge_body,
        grid=(t // tm,),
        in_specs=[row(y_ssm.shape[1]), row(y_sb.shape[1]), row(y_nsa.shape[1]),
                  pl.BlockSpec((tm, 3 * d), lambda i: (i, C_MG // (3 * d))),
                  row(d), full(w_ssm), full(w_sb), full(w_nsa_pad), full(w_out), full(gain)],
        out_specs=row(d),
        out_shape=jax.ShapeDtypeStruct((t, d), F32),
        compiler_params=_params("parallel"),
        name="merge",
    )(y_ssm, y_sb, y_nsa, proj2d, x2d, w_ssm, w_sb, w_nsa_pad, w_out, gain)


FFN_HALO = 16


def _ffn_body(x_ref, xh_ref, gin_ref, wup_ref, cw_ref, cb_ref, wd_ref, gout_ref, o_ref, *, tn, tiles_per_seq):
    i = pl.program_id(0)
    halo = FFN_HALO
    nj = D_FF // tn

    def norm(x):
        r = lax.rsqrt(jnp.mean(x * x, axis=-1, keepdims=True) + NORM_EPS)
        return (x * r * gin_ref[...]).astype(BF16)

    xh = jnp.where(i % tiles_per_seq == 0, 0.0, xh_ref[...])
    h = jnp.concatenate([norm(xh), norm(x_ref[...])], axis=0)

    def up(j):
        g0, v0 = j * tn, D_FF + j * tn
        return _dot(h, wup_ref[:, g0:g0 + tn]), _dot(h, wup_ref[:, v0:v0 + tn])

    def conv(u, c0):
        out = cb_ref[:, c0:c0 + tn] + cw_ref[FFN_CONV - 1:FFN_CONV, c0:c0 + tn] * u[halo:]
        for k in range(FFN_CONV - 1):
            out = out + cw_ref[k:k + 1, c0:c0 + tn] * pltpu.roll(u, FFN_CONV - 1 - k, axis=0)[halo:]
        return out

    acc = None
    nxt = up(0)
    for j in range(nj):
        ug, uv = nxt
        if j + 1 < nj:
            nxt = up(j + 1)
        cg = conv(ug, j * tn)
        cv = conv(uv, D_FF + j * tn)
        gelu = 0.5 * cg * (1.0 + jnp.tanh(0.7978845608028654 * (cg + 0.044715 * cg * cg * cg)))
        part = _dot((gelu * cv).astype(BF16), wd_ref[j * tn:(j + 1) * tn, :])
        acc = part if acc is None else acc + part
    r = lax.rsqrt(jnp.mean(acc * acc, axis=-1, keepdims=True) + NORM_EPS)
    o_ref[...] = x_ref[...] + acc * r * gout_ref[...]


def _ffn(x2d, s, gain_in, w_up, conv_w, conv_b, w_down, gain_out, tm=512, tn=256):
    t, d = x2d.shape
    halo = FFN_HALO
    cw = conv_w.astype(F32)
    cb = conv_b.reshape(1, -1).astype(F32)
    row1 = lambda a: a.reshape(1, d).astype(F32)
    full = lambda a: pl.BlockSpec(a.shape, lambda i: (0, 0))
    w_up = w_up.astype(BF16)
    w_down = w_down.astype(BF16)
    return pl.pallas_call(
        functools.partial(_ffn_body, tn=tn, tiles_per_seq=s // tm),
        grid=(t // tm,),
        in_specs=[pl.BlockSpec((tm, d), lambda i: (i, 0)),
                  pl.BlockSpec((halo, d), lambda i: (jnp.maximum(i * (tm // halo) - 1, 0), 0)),
                  pl.BlockSpec((1, d), lambda i: (0, 0)),
                  full(w_up), full(cw), full(cb), full(w_down),
                  pl.BlockSpec((1, d), lambda i: (0, 0))],
        out_specs=pl.BlockSpec((tm, d), lambda i: (i, 0)),
        out_shape=jax.ShapeDtypeStruct((t, d), F32),
        compiler_params=_params("parallel"),
        name="ffn",
    )(x2d, x2d, row1(gain_in), w_up, cw, cb, w_down, row1(gain_out))


def _pack_w_in(w):
    d = w.shape[0]
    o = 0
    z = w[:, o:o + 1024]; o += 1024
    xs = w[:, o:o + 1024]; o += 1024
    bc = w[:, o:o + 1024]; o += 1024
    dt = w[:, o:o + SSM_HEADS]; o += SSM_HEADS
    sb = w[:, o:o + 1536]; o += 1536
    n_sbq = SB_HEADS * SB_HEAD_DIM
    sb = jnp.concatenate([sb[:, :n_sbq] * SB_HEAD_DIM ** -0.5, sb[:, n_sbq:]], axis=1)
    nq = w[:, o:o + 512]; o += 512
    nkv = w[:, o:o + 768]; o += 768
    ng = w[:, o:o + 24]; o += 24
    mg = w[:, o:o + 3072]
    hd = NSA_HEAD_DIM
    pad64 = lambda a: jnp.pad(a.reshape(d, -1, hd), ((0, 0), (0, 0), (0, LANES - hd))).reshape(d, -1)
    nq_pad = pad64(nq * (NSA_HEAD_DIM ** -0.5 * LOG2E))
    kv_pad = pad64(nkv[:, 2 * NSA_KV_HEADS * hd:])
    cmp_raw = nkv[:, :2 * NSA_KV_HEADS * hd]
    main = jnp.concatenate([z, xs, bc, mg, nq_pad, kv_pad, sb], axis=1)
    ng3 = ng.reshape(d, 3, NSA_KV_HEADS, NSA_HG)
    gate_blocks = [jnp.pad(ng3[:, :, g, :].reshape(d, 3 * NSA_HG), ((0, 0), (0, LANES - 3 * NSA_HG)))
                   for g in range(NSA_KV_HEADS)]
    small = jnp.concatenate([cmp_raw, jnp.pad(dt, ((0, 0), (0, LANES - SSM_HEADS)))] + gate_blocks, axis=1)
    return main.astype(BF16), small.astype(BF16)


def kernel(x, pre_mix_norm, w_in, ssm_conv_w, ssm_conv_b, ssm_dt_bias, ssm_a_log, ssm_d, ssm_norm, cmp_pos_k, cmp_w1_k, cmp_w2_k, cmp_pos_v, cmp_w1_v, cmp_w2_v, w_br_ssm, w_br_sb, w_br_nsa, w_out, post_mix_norm, pre_ffn_norm, ffn_w_up, ffn_conv_w, ffn_conv_b, ffn_w_down, post_ffn_norm):
    b, s, d = x.shape
    t = b * s
    x2d = x.reshape(t, d)
    hd = NSA_HEAD_DIM
    for l in range(w_in.shape[0]):
        w_main, w_small = _pack_w_in(w_in[l])
        proj = _norm_matmul(x2d, pre_mix_norm[l], w_main, BF16, tm=1024, tn=2432).reshape(b, s, N_MAIN)
        small = _norm_matmul(x2d, pre_mix_norm[l], w_small, F32, tm=1024, tn=N_SMALL).reshape(b, s, N_SMALL)
        y_ssm = _ssd(proj, small, ssm_conv_w[l], ssm_conv_b[l], ssm_dt_bias[l], ssm_a_log[l], ssm_d[l], ssm_norm[l])
        y_sb = _sb_attention(proj)
        kvc = _nsa_compress(small, cmp_pos_k[l], cmp_w1_k[l], cmp_w2_k[l], cmp_pos_v[l], cmp_w1_v[l], cmp_w2_v[l])
        o_cmp, sel, flags = _nsa_cmp(proj, kvc, tq=NSA_TILE, tk=NSA_TILE)
        y_nsa = _nsa_main(proj, small, o_cmp, sel, flags, t=NSA_TILE)
        w_nsa_pad = jnp.pad(w_br_nsa[l].reshape(NSA_Q_HEADS, hd, d), ((0, 0), (0, LANES - hd), (0, 0)))
        x2d = _merge(x2d, proj.reshape(t, N_MAIN), y_ssm.reshape(t, -1), y_sb.reshape(t, -1), y_nsa.reshape(t, -1),
                     w_br_ssm[l].astype(BF16), w_br_sb[l].astype(BF16),
                     w_nsa_pad.reshape(NSA_Q_HEADS * LANES, d).astype(BF16), w_out[l].astype(BF16), post_mix_norm[l])
        x2d = _ffn(x2d, s, pre_ffn_norm[l], ffn_w_up[l], ffn_conv_w[l], ffn_conv_b[l], ffn_w_down[l],
                   post_ffn_norm[l])
    return x2d.reshape(b, s, d)
```
